```python
import functools
import jax, jax.numpy as jnp
from jax import lax
import numpy as np

D_MODEL = 1024
BATCH = 2
SEQ = 8192
DEPTH = 1
DEC_BATCH = 128
DEC_SEQ = 8
PAST_LEN = 8192
PAGE_SIZE = 128

D_MIX = D_MODEL
D_CONV = D_MIX // 2
D_SB = D_MIX // 2
SB_HEAD_DIM = 64
N_SB_HEADS = D_SB // SB_HEAD_DIM
SB_BIAS_INIT = -6.0
CONV_WIDTH = 3
N_MEM = 256
N_X_HEADS = 4
X_HEAD_DIM = D_MODEL // N_X_HEADS
D_X = N_X_HEADS * X_HEAD_DIM
D_FF = 2816
Q_BLOCK = 128
LN_EPS = 1e-5
DN_ALPHA = (2 * DEPTH) ** 0.25
DN_BETA = (8 * DEPTH) ** -0.25

kernel_name = 'hymba_conv_stickbreak_macaron_deepnorm_step'


def _layer_norm(x, g, b):
    xf = x.astype(jnp.float32)
    mu = jnp.mean(xf, axis=-1, keepdims=True)
    var = jnp.mean(jnp.square(xf - mu), axis=-1, keepdims=True)
    y = (xf - mu) * lax.rsqrt(var + LN_EPS) * g.astype(jnp.float32) + b.astype(jnp.float32)
    return y.astype(x.dtype)


def _swiglu(x, w_gu, w_down):
    gu = x @ w_gu
    g, u = gu[..., :D_FF], gu[..., D_FF:]
    return (jax.nn.silu(g) * u) @ w_down


def _stick_break(q, k, v, valid, bias):
    z = jnp.einsum('...hqd,...hkd->...hqk', q, k).astype(jnp.float32) * (SB_HEAD_DIM ** -0.5)
    z = z + bias.astype(jnp.float32)[:, None, None]
    log_stay = jnp.where(valid, jax.nn.log_sigmoid(-z), 0.0)
    after = lax.cumsum(log_stay, axis=z.ndim - 1, reverse=True) - log_stay
    w = jnp.where(valid, jnp.exp(jax.nn.log_sigmoid(z) + after), 0.0)
    return jnp.einsum('...hqk,...hkd->...hqd', w.astype(v.dtype), v)


def _sb_prompt(q, k, v, bias):
    b, s, h, d = q.shape
    qh, kh, vh = (t.transpose(0, 2, 1, 3) for t in (q, k, v))
    k_pos = jnp.arange(s)

    def block(i):
        qb = lax.dynamic_slice_in_dim(qh, i * Q_BLOCK, Q_BLOCK, axis=2)
        q_pos = i * Q_BLOCK + jnp.arange(Q_BLOCK)
        valid = k_pos[None, :] < q_pos[:, None]
        return _stick_break(qb, kh, vh, valid, bias)

    out = lax.map(block, jnp.arange(s // Q_BLOCK))
    return out.transpose(1, 0, 3, 2, 4).reshape(b, s, h, d)


def _sb_sample(q, k, v, bias, ck, cv, page_table):
    t = q.shape[1]
    past = page_table.shape[1] * ck.shape[1]
    q_pos = past + jnp.arange(t)
    k_pos = jnp.arange(past + t)
    valid = k_pos[None, :] < q_pos[:, None]

    def one(args):
        pt, qb, kb, vb = args
        kk = jnp.concatenate([ck[pt].reshape(past, N_SB_HEADS, SB_HEAD_DIM), kb], axis=0)
        vv = jnp.concatenate([cv[pt].reshape(past, N_SB_HEADS, SB_HEAD_DIM), vb], axis=0)
        o = _stick_break(qb.transpose(1, 0, 2), kk.transpose(1, 0, 2), vv.transpose(1, 0, 2), valid, bias)
        return o.transpose(1, 0, 2)

    return lax.map(one, (page_table, q, k, v))


def _short_conv(ctx, u, w):
    t = u.shape[1]
    full = jnp.concatenate([ctx, u], axis=1)
    z = sum(w[i] * full[:, i:i + t] for i in range(CONV_WIDTH))
    return z, full[:, -(CONV_WIDTH - 1):]


def _cross_attn(x, mk, mv, w_q, w_o):
    b, t, _ = x.shape
    q = (x @ w_q).reshape(b, t, N_X_HEADS, X_HEAD_DIM)
    s = jnp.einsum('bthd,bmhd->bhtm', q, mk).astype(jnp.float32) * (X_HEAD_DIM ** -0.5)
    a = jax.nn.softmax(s, axis=-1).astype(mv.dtype)
    o = jnp.einsum('bhtm,bmhd->bthd', a, mv).reshape(b, t, D_X)
    return o @ w_o


def _layer(x, lw, conv_ctx, sb_fn, mem_k, mem_v):
    (g, bb, w1_gu, w1_dn, w_in, conv_w, sb_b, w_out, w_xq, w_xo, w2_gu, w2_dn) = lw
    b, t, _ = x.shape
    x = _layer_norm(DN_ALPHA * x + 0.5 * _swiglu(x, w1_gu, w1_dn), g[0], bb[0])
    p = x @ w_in
    sizes = (D_CONV, D_CONV, D_CONV, D_SB, D_SB, D_SB)
    parts, o = [], 0
    for sz in sizes:
        parts.append(p[..., o:o + sz])
        o += sz
    gb, gc, hx, q, k, v = parts
    z, new_conv = _short_conv(conv_ctx, gc * hx, conv_w)
    y_conv = gb * z
    q = q.reshape(b, t, N_SB_HEADS, SB_HEAD_DIM)
    k = k.reshape(b, t, N_SB_HEADS, SB_HEAD_DIM)
    v = v.reshape(b, t, N_SB_HEADS, SB_HEAD_DIM)
    y_sb = sb_fn(q, k, v, sb_b).reshape(b, t, D_SB)
    mix = jnp.concatenate([y_conv, y_sb], axis=-1) @ w_out
    x = _layer_norm(DN_ALPHA * x + mix, g[1], bb[1])
    x = _layer_norm(DN_ALPHA * x + _cross_attn(x, mem_k, mem_v, w_xq, w_xo), g[2], bb[2])
    x = _layer_norm(DN_ALPHA * x + 0.5 * _swiglu(x, w2_gu, w2_dn), g[3], bb[3])
    return x, k, v, new_conv


def setup_inputs(seed: int = 0) -> dict:
    key = jax.random.key(seed)
    ks = jax.random.split(key, 24)
    f32 = jnp.float32
    n_pages = PAST_LEN // PAGE_SIZE
    n_used = DEC_BATCH * n_pages
    n_pool = n_used + max(1, n_used // 4)
    nrm = lambda k, shape, s: jax.random.normal(k, shape, f32) * s
    x_prompt = nrm(ks[0], (BATCH, SEQ, D_MODEL), 1.0)
    x_sample = nrm(ks[1], (DEC_BATCH, DEC_SEQ, D_MODEL), 1.0)
    mem_prompt = nrm(ks[2], (BATCH, N_MEM, D_MODEL), 1.0)
    cache_k = nrm(ks[3], (DEPTH, n_pool, PAGE_SIZE, N_SB_HEADS, SB_HEAD_DIM), 1.0)
    cache_v = nrm(ks[4], (DEPTH, n_pool, PAGE_SIZE, N_SB_HEADS, SB_HEAD_DIM), DN_BETA)
    state_conv = nrm(ks[5], (DEPTH, DEC_BATCH, CONV_WIDTH - 1, D_CONV), 1.0)
    cache_mem_k = nrm(ks[6], (DEPTH, DEC_BATCH, N_MEM, N_X_HEADS, X_HEAD_DIM), 1.0)
    cache_mem_v = nrm(ks[7], (DEPTH, DEC_BATCH, N_MEM, N_X_HEADS, X_HEAD_DIM), DN_BETA)
    page_table = jax.random.permutation(ks[8], n_pool)[:n_used].reshape(DEC_BATCH, n_pages).astype(jnp.int32)
    ln_g = 1.0 + nrm(ks[9], (DEPTH, 4, D_MODEL), 0.02)
    ln_b = nrm(ks[10], (DEPTH, 4, D_MODEL), 0.02)
    dm = D_MODEL ** -0.5
    w_ffn1_gu = nrm(ks[11], (DEPTH, D_MODEL, 2 * D_FF), dm)
    w_ffn1_down = nrm(ks[12], (DEPTH, D_FF, D_MODEL), D_FF ** -0.5 * DN_BETA)
    w_in_main = nrm(ks[13], (DEPTH, D_MODEL, 3 * D_CONV + 2 * D_SB), dm)
    w_in_v = nrm(ks[14], (DEPTH, D_MODEL, D_SB), dm * DN_BETA)
    w_in = jnp.concatenate([w_in_main, w_in_v], axis=-1)
    conv_w = nrm(ks[15], (DEPTH, CONV_WIDTH, D_CONV), CONV_WIDTH ** -0.5)
    sb_bias = SB_BIAS_INIT + nrm(ks[23], (DEPTH, N_SB_HEADS), 0.1)
    w_out = nrm(ks[16], (DEPTH, D_MIX, D_MODEL), D_MIX ** -0.5 * DN_BETA)
    w_xq = nrm(ks[17], (DEPTH, D_MODEL, D_X), dm)
    w_xk = nrm(ks[18], (DEPTH, D_MODEL, D_X), dm)
    w_xv = nrm(ks[19], (DEPTH, D_MODEL, D_X), dm * DN_BETA)
    w_xo = nrm(ks[20], (DEPTH, D_X, D_MODEL), D_X ** -0.5 * DN_BETA)
    w_ffn2_gu = nrm(ks[21], (DEPTH, D_MODEL, 2 * D_FF), dm)
    w_ffn2_down = nrm(ks[22], (DEPTH, D_FF, D_MODEL), D_FF ** -0.5 * DN_BETA)
    return {'x_prompt': x_prompt, 'x_sample': x_sample, 'mem_prompt': mem_prompt,
            'cache_k': cache_k, 'cache_v': cache_v, 'state_conv': state_conv,
            'cache_mem_k': cache_mem_k, 'cache_mem_v': cache_mem_v, 'page_table': page_table,
            'ln_g': ln_g, 'ln_b': ln_b, 'w_ffn1_gu': w_ffn1_gu, 'w_ffn1_down': w_ffn1_down,
            'w_in': w_in, 'conv_w': conv_w, 'sb_bias': sb_bias, 'w_out': w_out, 'w_xq': w_xq,
            'w_xk': w_xk, 'w_xv': w_xv, 'w_xo': w_xo, 'w_ffn2_gu': w_ffn2_gu, 'w_ffn2_down': w_ffn2_down}


def reference(x_prompt, x_sample, mem_prompt, cache_k, cache_v, state_conv, cache_mem_k, cache_mem_v,
              page_table, ln_g, ln_b, w_ffn1_gu, w_ffn1_down, w_in, conv_w, sb_bias, w_out, w_xq, w_xk,
              w_xv, w_xo, w_ffn2_gu, w_ffn2_down):
    yp, ys = x_prompt, x_sample
    bp = x_prompt.shape[0]
    kp_l, vp_l, cp_l, mkp_l, mvp_l, ks_l, vs_l, cs_l = [], [], [], [], [], [], [], []
    for l in range(DEPTH):
        lw = (ln_g[l], ln_b[l], w_ffn1_gu[l], w_ffn1_down[l], w_in[l], conv_w[l], sb_bias[l], w_out[l],
              w_xq[l], w_xo[l], w_ffn2_gu[l], w_ffn2_down[l])
        mk = (mem_prompt @ w_xk[l]).reshape(bp, N_MEM, N_X_HEADS, X_HEAD_DIM)
        mv = (mem_prompt @ w_xv[l]).reshape(bp, N_MEM, N_X_HEADS, X_HEAD_DIM)
        ctx0 = jnp.zeros((bp, CONV_WIDTH - 1, D_CONV), yp.dtype)
        yp, kp, vp, cp = _layer(yp, lw, ctx0, _sb_prompt, mk, mv)
        sb_s = functools.partial(_sb_sample, ck=cache_k[l], cv=cache_v[l], page_table=page_table)
        ys, ks, vs, cs = _layer(ys, lw, state_conv[l], sb_s, cache_mem_k[l], cache_mem_v[l])
        kp_l.append(kp); vp_l.append(vp); cp_l.append(cp); mkp_l.append(mk); mvp_l.append(mv)
        ks_l.append(ks); vs_l.append(vs); cs_l.append(cs)
    return (yp, ys, jnp.stack(kp_l), jnp.stack(vp_l), jnp.stack(cp_l), jnp.stack(mkp_l), jnp.stack(mvp_l),
            jnp.stack(ks_l), jnp.stack(vs_l), jnp.stack(cs_l))
```

```python
import functools

import jax
import jax.numpy as jnp
from jax import lax
from jax.experimental import pallas as pl
from jax.experimental.pallas import tpu as pltpu

F32 = jnp.float32
BF16 = jnp.bfloat16

LN_EPS = 1e-5
SB_HEAD_DIM = 64
X_HEAD_DIM = 256
CONV_WIDTH = 3
PAGE_SIZE = 128
SB_KEY_BLOCK = 256
VMEM_LIMIT = 56 * 1024 * 1024


def _dot(a, b):
    return jnp.dot(a, b, preferred_element_type=F32)


def _dot_nt(a, b):
    return lax.dot_general(a, b, (((1,), (1,)), ((), ())), preferred_element_type=F32)


def _layer_norm_rows(y, g, b):
    mu = jnp.mean(y, axis=-1, keepdims=True)
    d = y - mu
    var = jnp.mean(d * d, axis=-1, keepdims=True)
    return d * lax.rsqrt(var + LN_EPS) * g + b


def _resident(shape):
    nd = len(shape)
    return pl.BlockSpec(shape, lambda *_: (0,) * nd, pipeline_mode=pl.Buffered(1))


def _ffn_ln_kernel(x_ref, wgu_ref, wdn_ref, g_ref, b_ref, o_ref, *, d_ff, ff_chunk, alpha):
    x = x_ref[...]
    xb = x.astype(BF16)
    acc = jnp.zeros(x.shape, F32)
    for c in range(d_ff // ff_chunk):
        lo = c * ff_chunk
        gate = _dot(xb, wgu_ref[:, lo:lo + ff_chunk])
        up = _dot(xb, wgu_ref[:, d_ff + lo:d_ff + lo + ff_chunk])
        h = (gate / (1.0 + jnp.exp(-gate))) * up
        acc = acc + _dot(h.astype(BF16), wdn_ref[lo:lo + ff_chunk, :])
    y = alpha * x + 0.5 * acc
    o_ref[...] = _layer_norm_rows(y, g_ref[...], b_ref[...])


def _ffn_ln(x, wgu, wdn, g, b, *, alpha, tm, ff_chunk):
    m, d = x.shape
    d_ff = wdn.shape[0]
    kern = functools.partial(_ffn_ln_kernel, d_ff=d_ff, ff_chunk=ff_chunk, alpha=alpha)
    return pl.pallas_call(
        kern,
        grid=(m // tm,),
        in_specs=[pl.BlockSpec((tm, d), lambda i: (i, 0)),
                  _resident(wgu.shape), _resident(wdn.shape),
                  _resident(g.shape), _resident(b.shape)],
        out_specs=pl.BlockSpec((tm, d), lambda i: (i, 0)),
        out_shape=jax.ShapeDtypeStruct((m, d), F32),
        compiler_params=pltpu.CompilerParams(dimension_semantics=("arbitrary",),
                                             vmem_limit_bytes=VMEM_LIMIT),
        name="ffn_ln",
    )(x, wgu, wdn, g, b)


def _memproj_kernel(x_ref, wk_ref, wv_ref, k_ref, v_ref):
    xb = x_ref[...].astype(BF16)
    k_ref[...] = _dot(xb, wk_ref[...])
    v_ref[...] = _dot(xb, wv_ref[...])


def _memproj(x, wk, wv):
    m, d = x.shape
    n = wk.shape[1]
    return pl.pallas_call(
        _memproj_kernel,
        grid=(1,),
        in_specs=[_resident(x.shape), _resident(wk.shape), _resident(wv.shape)],
        out_specs=[pl.BlockSpec((m, n), lambda i: (0, 0)), pl.BlockSpec((m, n), lambda i: (0, 0))],
        out_shape=[jax.ShapeDtypeStruct((m, n), F32), jax.ShapeDtypeStruct((m, n), F32)],
        compiler_params=pltpu.CompilerParams(vmem_limit_bytes=VMEM_LIMIT),
        name="memproj",
    )(x, wk, wv)


def _conv_taps(u, prev2, prev1, cw_ref, seq_rows):
    r = u.shape[0]
    row = lax.broadcasted_iota(jnp.int32, (r, 1), 0)
    t = row if seq_rows is None else row % seq_rows
    r1 = pltpu.roll(u, 1, axis=0)
    r2 = pltpu.roll(u, 2, axis=0)
    back1 = jnp.where(t >= 1, r1, prev1)
    back2 = jnp.where(t >= 2, r2, jnp.where(t == 1, prev1, prev2))
    return cw_ref[0:1, :] * back2 + cw_ref[1:2, :] * back1 + cw_ref[2:3, :] * u


def _inproj_prompt_kernel(x_ref, w_ref, wkvt_ref, cw_ref, yc_ref, q_ref, kt_ref, vt_ref, ktb_ref, vtb_ref,
                          st_ref, carry_ref, *, dc, dsb):
    i = pl.program_id(1)

    @pl.when(i == 0)
    def _():
        carry_ref[...] = jnp.zeros(carry_ref.shape, F32)

    xb = x_ref[...].astype(BF16)
    p = _dot(xb, w_ref[...])
    gb, gc, hx = p[:, 0:dc], p[:, dc:2 * dc], p[:, 2 * dc:3 * dc]
    q = p[:, 3 * dc:3 * dc + dsb]
    u = gc * hx
    prev = carry_ref[...]
    z = _conv_taps(u, prev[6:7, :], prev[7:8, :], cw_ref, None)
    yc_ref[...] = (gb * z).astype(yc_ref.dtype)
    tail = u[u.shape[0] - 8:, :]
    carry_ref[...] = tail
    st_ref[...] = tail[6:8, :]
    q_ref[...] = (q * (SB_HEAD_DIM ** -0.5)).astype(q_ref.dtype)
    kvt = _dot_nt(wkvt_ref[...], xb)
    kt, vt = kvt[0:dsb, :], kvt[dsb:, :]
    kt_ref[...] = kt
    vt_ref[...] = vt
    ktb_ref[...] = kt.astype(BF16)
    vtb_ref[...] = vt.astype(BF16)


def _inproj_prompt(x, w_main, w_kv_t, conv_w, *, tm):
    bsz, s, d = x.shape
    dc = conv_w.shape[1]
    dsb = w_kv_t.shape[0] // 2
    kern = functools.partial(_inproj_prompt_kernel, dc=dc, dsb=dsb)
    row = lambda c: pl.BlockSpec((None, tm, c), lambda b, i: (b, i, 0))
    colm = pl.BlockSpec((None, dsb, tm), lambda b, i: (b, 0, i))
    return pl.pallas_call(
        kern,
        grid=(bsz, s // tm),
        in_specs=[row(d), _resident(w_main.shape), _resident(w_kv_t.shape), _resident(conv_w.shape)],
        out_specs=[row(dc), row(dsb), colm, colm, colm, colm,
                   pl.BlockSpec((None, CONV_WIDTH - 1, dc), lambda b, i: (b, 0, 0))],
        out_shape=[jax.ShapeDtypeStruct((bsz, s, dc), BF16),
                   jax.ShapeDtypeStruct((bsz, s, dsb), BF16),
                   jax.ShapeDtypeStruct((bsz, dsb, s), F32),
                   jax.ShapeDtypeStruct((bsz, dsb, s), F32),
                   jax.ShapeDtypeStruct((bsz, dsb, s), BF16),
                   jax.ShapeDtypeStruct((bsz, dsb, s), BF16),
                   jax.ShapeDtypeStruct((bsz, CONV_WIDTH - 1, dc), F32)],
        scratch_shapes=[pltpu.VMEM((8, dc), F32)],
        compiler_params=pltpu.CompilerParams(dimension_semantics=("arbitrary", "arbitrary"),
                                             vmem_limit_bytes=VMEM_LIMIT),
        name="inproj_prompt",
    )(x, w_main, w_kv_t, conv_w)


def _inproj_sample_kernel(x_ref, w_ref, cw_ref, c0_ref, c1_ref, yc_ref, q_ref, k_ref, v_ref, u_ref,
                          *, dc, dsb, seq_rows):
    p = _dot(x_ref[...].astype(BF16), w_ref[...])
    gb, gc, hx = p[:, 0:dc], p[:, dc:2 * dc], p[:, 2 * dc:3 * dc]
    o = 3 * dc
    u = gc * hx
    z = _conv_taps(u, c0_ref[...], c1_ref[...], cw_ref, seq_rows)
    yc_ref[...] = (gb * z).astype(yc_ref.dtype)
    u_ref[...] = u
    q_ref[...] = p[:, o:o + dsb] * (SB_HEAD_DIM ** -0.5)
    k_ref[...] = p[:, o + dsb:o + 2 * dsb]
    v_ref[...] = p[:, o + 2 * dsb:o + 3 * dsb]


def _inproj_sample(x, w_in, conv_w, c0, c1, *, seq_rows, tm):
    m, d = x.shape
    dc = conv_w.shape[1]
    dsb = (w_in.shape[1] - 3 * dc) // 3
    kern = functools.partial(_inproj_sample_kernel, dc=dc, dsb=dsb, seq_rows=seq_rows)
    row = lambda c: pl.BlockSpec((tm, c), lambda i: (i, 0))
    return pl.pallas_call(
        kern,
        grid=(m // tm,),
        in_specs=[row(d), _resident(w_in.shape), _resident(conv_w.shape), row(dc), row(dc)],
        out_specs=[row(dc), row(dsb), row(dsb), row(dsb), row(dc)],
        out_shape=[jax.ShapeDtypeStruct((m, dc), BF16),
                   jax.ShapeDtypeStruct((m, dsb), F32),
                   jax.ShapeDtypeStruct((m, dsb), F32),
                   jax.ShapeDtypeStruct((m, dsb), F32),
                   jax.ShapeDtypeStruct((m, dc), F32)],
        compiler_params=pltpu.CompilerParams(dimension_semantics=("arbitrary",),
                                             vmem_limit_bytes=VMEM_LIMIT),
        name="inproj_sample",
    )(x, w_in, conv_w, c0, c1)


def _neg_strict_upper(n):
    r = lax.broadcasted_iota(jnp.int32, (n, n), 0)
    c = lax.broadcasted_iota(jnp.int32, (n, n), 1)
    return jnp.where(r > c, -1.0, 0.0).astype(BF16)


def _sb_step(qb, kb, vb, bias, neg_upper, carry, acc, valid, transposed):
    z = (_dot(qb, kb) if transposed else _dot_nt(qb, kb)) + bias
    sp = jnp.maximum(z, 0.0) + jnp.log(1.0 + jnp.exp(-jnp.abs(z)))
    spm = sp if valid is None else jnp.where(valid, sp, 0.0)
    hi = spm.astype(BF16)
    lo = (spm - hi.astype(F32)).astype(BF16)
    after = _dot(hi, neg_upper) + _dot(lo, neg_upper) + carry
    w = jnp.exp(z - sp + after)
    if valid is not None:
        w = jnp.where(valid, w, 0.0)
    wb = w.astype(BF16)
    acc = acc + (_dot_nt(wb, vb) if transposed else _dot(wb, vb))
    carry = carry - jnp.sum(spm, axis=-1, keepdims=True)
    return carry, acc


def _sb_prompt_kernel(bias_ref, q_ref, k_ref, v_ref, o_ref, *, tq):
    tk = SB_KEY_BLOCK
    hp = pl.program_id(1)
    qi = pl.program_id(2)
    neg_upper = _neg_strict_upper(tk)
    q2 = q_ref[...]
    heads = (0, 1)
    qs = [q2[:, e * SB_HEAD_DIM:(e + 1) * SB_HEAD_DIM] for e in heads]
    biases = [bias_ref[2 * hp + e] for e in heads]
    row = lax.broadcasted_iota(jnp.int32, (tq, tk), 0)
    col = lax.broadcasted_iota(jnp.int32, (tq, tk), 1)

    def block(j, state, diag_offset):
        start = pl.multiple_of(j * tk, tk)
        out = []
        for e in heads:
            carry, acc = state[e]
            kb = k_ref[e * SB_HEAD_DIM:(e + 1) * SB_HEAD_DIM, pl.ds(start, tk)]
            vb = v_ref[e * SB_HEAD_DIM:(e + 1) * SB_HEAD_DIM, pl.ds(start, tk)]
            valid = None if diag_offset is None else (col + diag_offset < row)
            out.append(_sb_step(qs[e], kb, vb, biases[e], neg_upper, carry, acc, valid, True))
        return tuple(out)

    state = tuple((jnp.zeros((tq, 1), F32), jnp.zeros((tq, SB_HEAD_DIM), F32)) for _ in heads)
    n_diag = tq // tk
    for d in reversed(range(n_diag)):
        state = block(qi * n_diag + d, state, d * tk)
    state = lax.fori_loop(0, qi * n_diag, lambda jj, st: block(qi * n_diag - 1 - jj, st, None), state)
    o_ref[...] = jnp.concatenate([state[e][1] for e in heads], axis=-1).astype(o_ref.dtype)


def _sb_prompt(q, k_t, v_t, bias, *, tq):
    bsz, s, dsb = q.shape
    nh = dsb // SB_HEAD_DIM
    kern = functools.partial(_sb_prompt_kernel, tq=tq)
    kv_spec = pl.BlockSpec((None, 2 * SB_HEAD_DIM, s), lambda b, hp, qi: (b, hp, 0))
    return pl.pallas_call(
        kern,
        grid=(bsz, nh // 2, s // tq),
        in_specs=[pl.BlockSpec(memory_space=pltpu.SMEM),
                  pl.BlockSpec((None, tq, 2 * SB_HEAD_DIM), lambda b, hp, qi: (b, qi, hp)),
                  kv_spec, kv_spec],
        out_specs=pl.BlockSpec((None, tq, 2 * SB_HEAD_DIM), lambda b, hp, qi: (b, qi, hp)),
        out_shape=jax.ShapeDtypeStruct((bsz, s, dsb), BF16),
        compiler_params=pltpu.CompilerParams(
            dimension_semantics=("arbitrary", "arbitrary", "arbitrary"),
            vmem_limit_bytes=VMEM_LIMIT),
        name="sb_prompt",
    )(bias, q, k_t, v_t)


def _sb_sample_kernel(pt_ref, bias_ref, q_ref, kn_ref, vn_ref, ck_ref, cv_ref, o_ref,
                      kbuf, vbuf, sem, carry_ref, acc_ref, *, n_chunks, pages_per_chunk, t_new):
    tk = SB_KEY_BLOCK
    b = pl.program_id(0)
    c = pl.program_id(1)
    nb = pl.num_programs(0)
    step = b * n_chunks + c
    slot = step % 2
    dsb = q_ref.shape[-1]
    nh = dsb // SB_HEAD_DIM
    rows = nh * t_new

    def page_copies(bb, cc, sl):
        first = (n_chunks - 1 - cc) * pages_per_chunk
        cps = []
        for p in range(pages_per_chunk):
            page = pt_ref[bb, first + p]
            cps.append(pltpu.make_async_copy(ck_ref.at[page], kbuf.at[sl, p], sem.at[sl, 0]))
            cps.append(pltpu.make_async_copy(cv_ref.at[page], vbuf.at[sl, p], sem.at[sl, 1]))
        return cps

    @pl.when(step == 0)
    def _():
        for cp in page_copies(b, c, slot):
            cp.start()

    @pl.when(step + 1 < nb * n_chunks)
    def _():
        nxt = step + 1
        for cp in page_copies(nxt // n_chunks, nxt % n_chunks, 1 - slot):
            cp.start()

    rrow = lax.broadcasted_iota(jnp.int32, (rows, dsb), 0)
    rcol = lax.broadcasted_iota(jnp.int32, (rows, dsb), 1)
    head_mask = (rrow // t_new) == (rcol // SB_HEAD_DIM)
    q_rep = jnp.concatenate([q_ref[...]] * nh, axis=0)
    qbd = jnp.where(head_mask, q_rep, 0.0).astype(BF16)
    bias = bias_ref[:, 0:1]
    neg_upper = _neg_strict_upper(tk)

    @pl.when(c == 0)
    def _():
        pad = jnp.zeros((128 - t_new, dsb), BF16)
        kb = jnp.concatenate([kn_ref[...].astype(BF16), pad], axis=0)
        vb = jnp.concatenate([vn_ref[...].astype(BF16), pad], axis=0)
        t_of_row = lax.broadcasted_iota(jnp.int32, (rows, 128), 0) % t_new
        key = lax.broadcasted_iota(jnp.int32, (rows, 128), 1)
        carry, acc = _sb_step(qbd, kb, vb, bias, neg_upper[:128, :128],
                              jnp.zeros((rows, 1), F32), jnp.zeros((rows, dsb), F32), key < t_of_row,
                              False)
        carry_ref[...] = jnp.broadcast_to(carry, carry_ref.shape)
        acc_ref[...] = acc

    for cp in page_copies(b, c, slot):
        cp.wait()

    pages_per_block = tk // PAGE_SIZE
    n_blocks = pages_per_chunk // pages_per_block

    def body(jj, state):
        carry, acc = state
        first = (n_blocks - 1 - jj) * pages_per_block
        kb = jnp.concatenate([kbuf[slot, first + p] for p in range(pages_per_block)], axis=1)
        vb = jnp.concatenate([vbuf[slot, first + p] for p in range(pages_per_block)], axis=1)
        return _sb_step(qbd, kb.astype(BF16), vb.astype(BF16), bias, neg_upper, carry, acc, None, True)

    carry, acc = lax.fori_loop(0, n_blocks, body, (carry_ref[:, 0:1], acc_ref[...]))
    carry_ref[...] = jnp.broadcast_to(carry, carry_ref.shape)
    acc_ref[...] = acc

    @pl.when(c == n_chunks - 1)
    def _():
        own = jnp.where(head_mask, acc, 0.0).reshape(nh, t_new, dsb)
        o_ref[...] = jnp.sum(own, axis=0)


def _sb_sample(q, k_new, v_new, cache_k, cache_v, page_table, bias_rows, *, t_new, n_chunks):
    m, dsb = q.shape
    nbatch = m // t_new
    n_pages = page_table.shape[1]
    ppc = n_pages // n_chunks
    rows = bias_rows.shape[0]
    kern = functools.partial(_sb_sample_kernel, n_chunks=n_chunks, pages_per_chunk=ppc, t_new=t_new)
    row = pl.BlockSpec((t_new, dsb), lambda b, c, pt: (b, 0))
    grid_spec = pltpu.PrefetchScalarGridSpec(
        num_scalar_prefetch=1,
        grid=(nbatch, n_chunks),
        in_specs=[pl.BlockSpec(bias_rows.shape, lambda b, c, pt: (0, 0)), row, row, row,
                  pl.BlockSpec(memory_space=pl.ANY), pl.BlockSpec(memory_space=pl.ANY)],
        out_specs=row,
        scratch_shapes=[pltpu.VMEM((2, ppc, dsb, PAGE_SIZE), F32),
                        pltpu.VMEM((2, ppc, dsb, PAGE_SIZE), F32),
                        pltpu.SemaphoreType.DMA((2, 2)),
                        pltpu.VMEM((rows, 128), F32),
                        pltpu.VMEM((rows, dsb), F32)])
    return pl.pallas_call(
        kern,
        grid_spec=grid_spec,
        out_shape=jax.ShapeDtypeStruct((m, dsb), F32),
        compiler_params=pltpu.CompilerParams(dimension_semantics=("arbitrary", "arbitrary"),
                                             vmem_limit_bytes=VMEM_LIMIT),
        name="sb_sample",
    )(page_table, bias_rows, q, k_new, v_new, cache_k, cache_v)


def _mix_xattn_kernel(x_ref, yc_ref, ysb_ref, wo_ref, wq_ref, wxo_ref, mk_ref, mv_ref, g_ref, b_ref,
                      o_ref, *, alpha, n_mem_batches):
    x = x_ref[...]
    dc = yc_ref.shape[-1]
    mix = (_dot(yc_ref[...].astype(BF16), wo_ref[0:dc, :])
           + _dot(ysb_ref[...].astype(BF16), wo_ref[dc:, :]))
    x2 = _layer_norm_rows(alpha * x + mix, g_ref[1:2, :], b_ref[1:2, :])
    q = _dot(x2.astype(BF16), wq_ref[...]).astype(BF16)
    tm, dx = q.shape
    rpb = tm // n_mem_batches
    n_heads = dx // X_HEAD_DIM
    outs = []
    for j in range(n_mem_batches):
        mk = mk_ref[j].astype(BF16)
        mv = mv_ref[j].astype(BF16)
        qj = q[j * rpb:(j + 1) * rpb, :]
        heads = []
        for h in range(n_heads):
            hs = slice(h * X_HEAD_DIM, (h + 1) * X_HEAD_DIM)
            s = _dot_nt(qj[:, hs], mk[:, hs]) * (X_HEAD_DIM ** -0.5)
            e = jnp.exp(s - jnp.max(s, axis=-1, keepdims=True))
            a = e / jnp.sum(e, axis=-1, keepdims=True)
            heads.append(_dot(a.astype(BF16), mv[:, hs]))
        outs.append(jnp.concatenate(heads, axis=-1))
    o = outs[0] if n_mem_batches == 1 else jnp.concatenate(outs, axis=0)
    xo = _dot(o.astype(BF16), wxo_ref[...])
    o_ref[...] = _layer_norm_rows(alpha * x2 + xo, g_ref[2:3, :], b_ref[2:3, :])


def _mix_xattn(x, yc, ysb, w_out, w_xq, w_xo, mk, mv, g, b, *, alpha, tm, rows_per_mem):
    m, d = x.shape
    nbm = tm // rows_per_mem
    n_mem, dx = mk.shape[1], mk.shape[2]
    kern = functools.partial(_mix_xattn_kernel, alpha=alpha, n_mem_batches=nbm)
    row = lambda c: pl.BlockSpec((tm, c), lambda i: (i, 0))
    if rows_per_mem >= tm:
        mem_spec = pl.BlockSpec((1, n_mem, dx), lambda i: (i * tm // rows_per_mem, 0, 0))
        nbm = 1
        kern = functools.partial(_mix_xattn_kernel, alpha=alpha, n_mem_batches=1)
    else:
        mem_spec = pl.BlockSpec((nbm, n_mem, dx), lambda i: (i, 0, 0))
    return pl.pallas_call(
        kern,
        grid=(m // tm,),
        in_specs=[row(d), row(yc.shape[1]), row(ysb.shape[1]),
                  _resident(w_out.shape), _resident(w_xq.shape), _resident(w_xo.shape),
                  mem_spec, mem_spec, _resident(g.shape), _resident(b.shape)],
        out_specs=row(d),
        out_shape=jax.ShapeDtypeStruct((m, d), F32),
        compiler_params=pltpu.CompilerParams(dimension_semantics=("arbitrary",),
                                             vmem_limit_bytes=VMEM_LIMIT),
        name="mix_xattn",
    )(x, yc, ysb, w_out, w_xq, w_xo, mk, mv, g, b)


def _tile(m, want):
    return want if m % want == 0 else m


def kernel(x_prompt, x_sample, mem_prompt, cache_k, cache_v, state_conv, cache_mem_k, cache_mem_v, page_table, ln_g, ln_b, w_ffn1_gu, w_ffn1_down, w_in, conv_w, sb_bias, w_out, w_xq, w_xk, w_xv, w_xo, w_ffn2_gu, w_ffn2_down):
    depth = ln_g.shape[0]
    alpha = (2 * depth) ** 0.25
    bp, s, d = x_prompt.shape
    db, t_new, _ = x_sample.shape
    n_mem = mem_prompt.shape[1]
    dc = conv_w.shape[-1]
    dsb = (w_in.shape[-1] - 3 * dc) // 3
    nh = dsb // SB_HEAD_DIM
    d_ff = w_ffn1_down.shape[1]
    ff_chunk = d_ff // 2 if (d_ff // 2) % 128 == 0 else d_ff
    page = cache_k.shape[2]

    yp = x_prompt.reshape(bp * s, d)
    ys = x_sample.reshape(db * t_new, d)
    outs = {n: [] for n in ("kp", "vp", "cp", "mkp", "mvp", "ks", "vs", "cs")}
    for l in range(depth):
        g, b = ln_g[l], ln_b[l]
        w1gu, w1dn = w_ffn1_gu[l].astype(BF16), w_ffn1_down[l].astype(BF16)
        w2gu, w2dn = w_ffn2_gu[l].astype(BF16), w_ffn2_down[l].astype(BF16)
        win, wout = w_in[l].astype(BF16), w_out[l].astype(BF16)
        wxq, wxk, wxv, wxo = (w[l].astype(BF16) for w in (w_xq, w_xk, w_xv, w_xo))
        cw = conv_w[l]
        bias = sb_bias[l].astype(F32)
        ffn = functools.partial(_ffn_ln, alpha=alpha, ff_chunk=ff_chunk)

        mk, mv = _memproj(mem_prompt.reshape(bp * n_mem, d), wxk, wxv)
        x1 = ffn(yp, w1gu, w1dn, g[0:1], b[0:1], tm=_tile(bp * s, 512))
        yc, q, kt, vt, kt_bf, vt_bf, cst = _inproj_prompt(
            x1.reshape(bp, s, d), win[:, :3 * dc + dsb], win[:, 3 * dc + dsb:].T, cw, tm=_tile(s, 512))
        ysb = _sb_prompt(q, kt_bf, vt_bf, bias, tq=_tile(s, SB_KEY_BLOCK))
        tm = _tile(s, 512)
        x3 = _mix_xattn(x1, yc.reshape(bp * s, dc), ysb.reshape(bp * s, dsb), wout, wxq, wxo,
                        mk.reshape(bp, n_mem, -1), mv.reshape(bp, n_mem, -1), g, b,
                        alpha=alpha, tm=tm, rows_per_mem=s)
        yp = ffn(x3, w2gu, w2dn, g[3:4], b[3:4], tm=_tile(bp * s, 512))
        outs["kp"].append(kt.reshape(bp, nh, SB_HEAD_DIM, s).transpose(0, 3, 1, 2))
        outs["vp"].append(vt.reshape(bp, nh, SB_HEAD_DIM, s).transpose(0, 3, 1, 2))
        outs["cp"].append(cst)
        outs["mkp"].append(mk.reshape(bp, n_mem, -1, X_HEAD_DIM))
        outs["mvp"].append(mv.reshape(bp, n_mem, -1, X_HEAD_DIM))

        ms = db * t_new
        x1 = ffn(ys, w1gu, w1dn, g[0:1], b[0:1], tm=_tile(ms, 512))
        c0 = jnp.repeat(state_conv[l][:, 0, :], t_new, axis=0)
        c1 = jnp.repeat(state_conv[l][:, 1, :], t_new, axis=0)
        yc, q, k, v, u = _inproj_sample(x1, win, cw, c0, c1, seq_rows=t_new, tm=_tile(ms, 512))
        bias_rows = jnp.broadcast_to(jnp.repeat(bias, t_new)[:, None], (nh * t_new, 128))
        ck_t = cache_k[l].transpose(0, 2, 3, 1).reshape(-1, dsb, page)
        cv_t = cache_v[l].transpose(0, 2, 3, 1).reshape(-1, dsb, page)
        ysb = _sb_sample(q, k, v, ck_t, cv_t, page_table, bias_rows, t_new=t_new, n_chunks=2)
        mem_per_step = 8
        x3 = _mix_xattn(x1, yc, ysb, wout, wxq, wxo,
                        cache_mem_k[l].reshape(db, n_mem, -1), cache_mem_v[l].reshape(db, n_mem, -1),
                        g, b, alpha=alpha, tm=_tile(ms, mem_per_step * t_new), rows_per_mem=t_new)
        ys = ffn(x3, w2gu, w2dn, g[3:4], b[3:4], tm=_tile(ms, 512))
        outs["ks"].append(k.reshape(db, t_new, nh, SB_HEAD_DIM))
        outs["vs"].append(v.reshape(db, t_new, nh, SB_HEAD_DIM))
        outs["cs"].append(u.reshape(db, t_new, dc)[:, t_new - (CONV_WIDTH - 1):, :])

    st = lambda n: jnp.stack(outs[n])
    return (yp.reshape(bp, s, d), ys.reshape(db, t_new, d), st("kp"), st("vp"), st("cp"), st("mkp"),
            st("mvp"), st("ks"), st("vs"), st("cs"))
```

```python
import functools

import jax
import jax.numpy as jnp
from jax import lax
from jax.experimental import pallas as pl
from jax.experimental.pallas import tpu as pltpu

F32 = jnp.float32
BF16 = jnp.bfloat16

LN_EPS = 1e-5
NEG_LOG2_E = -1.4426950408889634
SB_HEAD_DIM = 64
X_HEAD_DIM = 256
CONV_WIDTH = 3
PAGE_SIZE = 128
SB_KEY_BLOCK = 256
SAMPLE_GROUP = 4
VMEM_LIMIT = 56 * 1024 * 1024


def _dot(a, b):
    return jnp.dot(a, b, preferred_element_type=F32)


def _dot_nt(a, b):
    return lax.dot_general(a, b, (((1,), (1,)), ((), ())), preferred_element_type=F32)


def _layer_norm_rows(y, g, b):
    mu = jnp.mean(y, axis=-1, keepdims=True)
    d = y - mu
    var = jnp.mean(d * d, axis=-1, keepdims=True)
    return d * lax.rsqrt(var + LN_EPS) * g + b


def _resident(shape):
    nd = len(shape)
    return pl.BlockSpec(shape, lambda *_: (0,) * nd, pipeline_mode=pl.Buffered(1))


def _ffn_ln_kernel(x_ref, wgu_ref, wdn_ref, g_ref, b_ref, o_ref, *, d_ff, ff_chunk, alpha):
    x = x_ref[...]
    xb = x.astype(BF16)
    acc = jnp.zeros(x.shape, F32)
    for c in range(d_ff // ff_chunk):
        lo = c * ff_chunk
        gate = _dot(xb, wgu_ref[:, lo:lo + ff_chunk])
        up = _dot(xb, wgu_ref[:, d_ff + lo:d_ff + lo + ff_chunk])
        h = (gate / (1.0 + jnp.exp(-gate))) * up
        acc = acc + _dot(h.astype(BF16), wdn_ref[lo:lo + ff_chunk, :])
    y = alpha * x + 0.5 * acc
    o_ref[...] = _layer_norm_rows(y, g_ref[...], b_ref[...])


def _ffn_ln(x, wgu, wdn, g, b, *, alpha, tm, ff_chunk):
    m, d = x.shape
    d_ff = wdn.shape[0]
    kern = functools.partial(_ffn_ln_kernel, d_ff=d_ff, ff_chunk=ff_chunk, alpha=alpha)
    return pl.pallas_call(
        kern,
        grid=(m // tm,),
        in_specs=[pl.BlockSpec((tm, d), lambda i: (i, 0)),
                  _resident(wgu.shape), _resident(wdn.shape),
                  _resident(g.shape), _resident(b.shape)],
        out_specs=pl.BlockSpec((tm, d), lambda i: (i, 0)),
        out_shape=jax.ShapeDtypeStruct((m, d), F32),
        compiler_params=pltpu.CompilerParams(dimension_semantics=("arbitrary",),
                                             vmem_limit_bytes=VMEM_LIMIT),
        name="ffn_ln",
    )(x, wgu, wdn, g, b)


def _memproj_kernel(x_ref, wk_ref, wv_ref, k_ref, v_ref):
    xb = x_ref[...].astype(BF16)
    k_ref[...] = _dot(xb, wk_ref[...])
    v_ref[...] = _dot(xb, wv_ref[...])


def _memproj(x, wk, wv):
    m, d = x.shape
    n = wk.shape[1]
    return pl.pallas_call(
        _memproj_kernel,
        grid=(1,),
        in_specs=[_resident(x.shape), _resident(wk.shape), _resident(wv.shape)],
        out_specs=[pl.BlockSpec((m, n), lambda i: (0, 0)), pl.BlockSpec((m, n), lambda i: (0, 0))],
        out_shape=[jax.ShapeDtypeStruct((m, n), F32), jax.ShapeDtypeStruct((m, n), F32)],
        compiler_params=pltpu.CompilerParams(vmem_limit_bytes=VMEM_LIMIT),
        name="memproj",
    )(x, wk, wv)


def _conv_taps(u, prev2, prev1, cw_ref, seq_rows):
    r = u.shape[0]
    row = lax.broadcasted_iota(jnp.int32, (r, 1), 0)
    t = row if seq_rows is None else row % seq_rows
    r1 = pltpu.roll(u, 1, axis=0)
    r2 = pltpu.roll(u, 2, axis=0)
    back1 = jnp.where(t >= 1, r1, prev1)
    back2 = jnp.where(t >= 2, r2, jnp.where(t == 1, prev1, prev2))
    return cw_ref[0:1, :] * back2 + cw_ref[1:2, :] * back1 + cw_ref[2:3, :] * u


def _inproj_prompt_kernel(x_ref, w_ref, wkvt_ref, cw_ref, yc_ref, q_ref, kt_ref, vt_ref, ktb_ref, vtb_ref,
                          st_ref, carry_ref, *, dc, dsb):
    i = pl.program_id(1)

    @pl.when(i == 0)
    def _():
        carry_ref[...] = jnp.zeros(carry_ref.shape, F32)

    xb = x_ref[...].astype(BF16)
    p = _dot(xb, w_ref[...])
    gb, gc, hx = p[:, 0:dc], p[:, dc:2 * dc], p[:, 2 * dc:3 * dc]
    q = p[:, 3 * dc:3 * dc + dsb]
    u = gc * hx
    prev = carry_ref[...]
    z = _conv_taps(u, prev[6:7, :], prev[7:8, :], cw_ref, None)
    yc_ref[...] = (gb * z).astype(yc_ref.dtype)
    tail = u[u.shape[0] - 8:, :]
    carry_ref[...] = tail
    st_ref[...] = tail[6:8, :]
    q_ref[...] = (q * (SB_HEAD_DIM ** -0.5)).astype(q_ref.dtype)
    kvt = _dot_nt(wkvt_ref[...], xb)
    kt, vt = kvt[0:dsb, :], kvt[dsb:, :]
    kt_ref[...] = kt
    vt_ref[...] = vt
    ktb_ref[...] = kt.astype(BF16)
    vtb_ref[...] = vt.astype(BF16)


def _inproj_prompt(x, w_main, w_kv_t, conv_w, *, tm):
    bsz, s, d = x.shape
    dc = conv_w.shape[1]
    dsb = w_kv_t.shape[0] // 2
    kern = functools.partial(_inproj_prompt_kernel, dc=dc, dsb=dsb)
    row = lambda c: pl.BlockSpec((None, tm, c), lambda b, i: (b, i, 0))
    colm = pl.BlockSpec((None, dsb, tm), lambda b, i: (b, 0, i))
    return pl.pallas_call(
        kern,
        grid=(bsz, s // tm),
        in_specs=[row(d), _resident(w_main.shape), _resident(w_kv_t.shape), _resident(conv_w.shape)],
        out_specs=[row(dc), row(dsb), colm, colm, colm, colm,
                   pl.BlockSpec((None, CONV_WIDTH - 1, dc), lambda b, i: (b, 0, 0))],
        out_shape=[jax.ShapeDtypeStruct((bsz, s, dc), BF16),
                   jax.ShapeDtypeStruct((bsz, s, dsb), BF16),
                   jax.ShapeDtypeStruct((bsz, dsb, s), F32),
                   jax.ShapeDtypeStruct((bsz, dsb, s), F32),
                   jax.ShapeDtypeStruct((bsz, dsb, s), BF16),
                   jax.ShapeDtypeStruct((bsz, dsb, s), BF16),
                   jax.ShapeDtypeStruct((bsz, CONV_WIDTH - 1, dc), F32)],
        scratch_shapes=[pltpu.VMEM((8, dc), F32)],
        compiler_params=pltpu.CompilerParams(dimension_semantics=("arbitrary", "arbitrary"),
                                             vmem_limit_bytes=VMEM_LIMIT),
        name="inproj_prompt",
    )(x, w_main, w_kv_t, conv_w)


def _inproj_sample_kernel(x_ref, w_ref, cw_ref, c0_ref, c1_ref, yc_ref, q_ref, k_ref, v_ref, u_ref,
                          *, dc, dsb, seq_rows):
    p = _dot(x_ref[...].astype(BF16), w_ref[...])
    gb, gc, hx = p[:, 0:dc], p[:, dc:2 * dc], p[:, 2 * dc:3 * dc]
    o = 3 * dc
    u = gc * hx
    z = _conv_taps(u, c0_ref[...], c1_ref[...], cw_ref, seq_rows)
    yc_ref[...] = (gb * z).astype(yc_ref.dtype)
    u_ref[...] = u
    q_ref[...] = p[:, o:o + dsb] * (SB_HEAD_DIM ** -0.5)
    k_ref[...] = p[:, o + dsb:o + 2 * dsb]
    v_ref[...] = p[:, o + 2 * dsb:o + 3 * dsb]


def _inproj_sample(x, w_in, conv_w, c0, c1, *, seq_rows, tm):
    m, d = x.shape
    dc = conv_w.shape[1]
    dsb = (w_in.shape[1] - 3 * dc) // 3
    kern = functools.partial(_inproj_sample_kernel, dc=dc, dsb=dsb, seq_rows=seq_rows)
    row = lambda c: pl.BlockSpec((tm, c), lambda i: (i, 0))
    return pl.pallas_call(
        kern,
        grid=(m // tm,),
        in_specs=[row(d), _resident(w_in.shape), _resident(conv_w.shape), row(dc), row(dc)],
        out_specs=[row(dc), row(dsb), row(dsb), row(dsb), row(dc)],
        out_shape=[jax.ShapeDtypeStruct((m, dc), BF16),
                   jax.ShapeDtypeStruct((m, dsb), F32),
                   jax.ShapeDtypeStruct((m, dsb), F32),
                   jax.ShapeDtypeStruct((m, dsb), F32),
                   jax.ShapeDtypeStruct((m, dc), F32)],
        compiler_params=pltpu.CompilerParams(dimension_semantics=("arbitrary",),
                                             vmem_limit_bytes=VMEM_LIMIT),
        name="inproj_sample",
    )(x, w_in, conv_w, c0, c1)


def _neg_strict_upper2(n):
    r = lax.broadcasted_iota(jnp.int32, (2 * n, n), 0)
    c = lax.broadcasted_iota(jnp.int32, (2 * n, n), 1)
    r = jnp.where(r >= n, r - n, r)
    return jnp.where(r > c, -1.0, 0.0).astype(BF16)


def _sb_scores(qb, kb, bias, transposed):
    return (_dot(qb, kb) if transposed else _dot_nt(qb, kb)) + bias


def _sb_softplus(z, valid):
    sp = jnp.maximum(z, 0.0) + jnp.log(1.0 + jnp.exp2(jnp.abs(z) * NEG_LOG2_E))
    spm = sp if valid is None else jnp.where(valid, sp, 0.0)
    hi = spm.astype(BF16)
    lo = (spm - hi.astype(F32)).astype(BF16)
    return z - sp, spm, jnp.concatenate([hi, lo], axis=1)


def _sb_weights(log_beta, spm, neg_later, carry, valid):
    w = jnp.exp(log_beta + neg_later + jnp.tile(carry, (1, log_beta.shape[1] // carry.shape[1])))
    if valid is not None:
        w = jnp.where(valid, w, 0.0)
    return w.astype(BF16), carry - jnp.sum(spm, axis=-1, keepdims=True)


def _sb_pv(wb, vb, acc, transposed):
    return acc + (_dot_nt(wb, vb) if transposed else _dot(wb, vb))


def _sb_step(qb, kb, vb, bias, neg_upper2, carry, acc, valid, transposed):
    log_beta, spm, hilo = _sb_softplus(_sb_scores(qb, kb, bias, transposed), valid)
    wb, carry = _sb_weights(log_beta, spm, _dot(hilo, neg_upper2), carry, valid)
    return carry, _sb_pv(wb, vb, acc, transposed)


def _sb_prompt_kernel(bias_ref, q_ref, k_ref, v_ref, o_ref, z_scr, w_scr, carry_scr, acc_scr, *, tq, n_heads):
    tk = SB_KEY_BLOCK
    hg = pl.program_id(1)
    qi = pl.program_id(2)
    neg_upper2 = _neg_strict_upper2(tk)
    qall = q_ref[...]
    heads = tuple(range(n_heads))
    qs = [qall[:, e * SB_HEAD_DIM:(e + 1) * SB_HEAD_DIM] for e in heads]
    biases = [bias_ref[n_heads * hg + e] for e in heads]
    row = lax.broadcasted_iota(jnp.int32, (tq, tk), 0)
    col = lax.broadcasted_iota(jnp.int32, (tq, tk), 1)

    def head_rows(ref, e, j):
        return ref[e * SB_HEAD_DIM:(e + 1) * SB_HEAD_DIM, pl.ds(pl.multiple_of(j * tk, tk), tk)]

    def scores(e, j):
        return _sb_scores(qs[e], head_rows(k_ref, e, j), biases[e], True)

    def middle(valid, j_next):
        parts = []
        for e in heads:
            log_beta, spm, hilo = _sb_softplus(z_scr[e], valid)
            parts.append((log_beta, spm, _dot(hilo, neg_upper2)))
            z_scr[e] = scores(e, j_next)
        for e, (log_beta, spm, neg_later) in zip(heads, parts):
            w_scr[e], carry_scr[e] = _sb_weights(log_beta, spm, neg_later, carry_scr[e], valid)

    def add_values(j):
        for e in heads:
            acc_scr[e] = _sb_pv(w_scr[e], head_rows(v_ref, e, j), acc_scr[e], True)

    for e in heads:
        z_scr[e] = scores(e, qi)
    carry_scr[...] = jnp.zeros(carry_scr.shape, F32)
    acc_scr[...] = jnp.zeros(acc_scr.shape, F32)
    middle(col < row, jnp.maximum(qi - 1, 0))

    def body(jj, _):
        j = qi - 1 - jj
        add_values(j + 1)
        middle(None, jnp.maximum(j - 1, 0))
        return 0

    lax.fori_loop(0, qi, body, 0)
    add_values(0)
    o_ref[...] = jnp.concatenate([acc_scr[e] for e in heads], axis=-1).astype(o_ref.dtype)


def _sb_prompt(q, k_t, v_t, bias, *, tq, heads_per_step):
    bsz, s, dsb = q.shape
    nh = dsb // SB_HEAD_DIM
    assert tq == SB_KEY_BLOCK and s % tq == 0 and nh % heads_per_step == 0
    gw = heads_per_step * SB_HEAD_DIM
    kern = functools.partial(_sb_prompt_kernel, tq=tq, n_heads=heads_per_step)
    kv_spec = pl.BlockSpec((None, gw, s), lambda b, hg, qi: (b, hg, 0))
    return pl.pallas_call(
        kern,
        grid=(bsz, nh // heads_per_step, s // tq),
        in_specs=[pl.BlockSpec(memory_space=pltpu.SMEM),
                  pl.BlockSpec((None, tq, gw), lambda b, hg, qi: (b, qi, hg)),
                  kv_spec, kv_spec],
        out_specs=pl.BlockSpec((None, tq, gw), lambda b, hg, qi: (b, qi, hg)),
        out_shape=jax.ShapeDtypeStruct((bsz, s, dsb), BF16),
        scratch_shapes=[pltpu.VMEM((heads_per_step, tq, SB_KEY_BLOCK), F32),
                        pltpu.VMEM((heads_per_step, tq, SB_KEY_BLOCK), BF16),
                        pltpu.VMEM((heads_per_step, tq, 128), F32),
                        pltpu.VMEM((heads_per_step, tq, SB_HEAD_DIM), F32)],
        compiler_params=pltpu.CompilerParams(
            dimension_semantics=("arbitrary", "arbitrary", "arbitrary"),
            vmem_limit_bytes=VMEM_LIMIT),
        name="sb_prompt",
    )(bias, q, k_t, v_t)


def _sb_sample_kernel(pt_ref, bias_ref, q_ref, kn_ref, vn_ref, ck_ref, cv_ref, o_ref,
                      kbuf, vbuf, sem, carry_ref, acc_ref, *, n_chunks, pages_per_chunk, t_new):
    tk = SB_KEY_BLOCK
    b = pl.program_id(0)
    c = pl.program_id(1)
    nb = pl.num_programs(0)
    step = b * n_chunks + c
    slot = step % 2
    dsb = q_ref.shape[-1]
    nh = dsb // SB_HEAD_DIM
    rows = nh * t_new

    def page_copies(bb, cc, sl):
        first = (n_chunks - 1 - cc) * pages_per_chunk
        cps = []
        for p in range(pages_per_chunk):
            page = pt_ref[bb, first + p]
            cps.append(pltpu.make_async_copy(ck_ref.at[page], kbuf.at[sl, p], sem.at[sl, 0]))
            cps.append(pltpu.make_async_copy(cv_ref.at[page], vbuf.at[sl, p], sem.at[sl, 1]))
        return cps

    @pl.when(step == 0)
    def _():
        for cp in page_copies(b, c, slot):
            cp.start()

    @pl.when(step + 1 < nb * n_chunks)
    def _():
        nxt = step + 1
        for cp in page_copies(nxt // n_chunks, nxt % n_chunks, 1 - slot):
            cp.start()

    rrow = lax.broadcasted_iota(jnp.int32, (rows, dsb), 0)
    rcol = lax.broadcasted_iota(jnp.int32, (rows, dsb), 1)
    head_mask = (rrow // t_new) == (rcol // SB_HEAD_DIM)
    q_rep = jnp.concatenate([q_ref[...]] * nh, axis=0)
    qbd = jnp.where(head_mask, q_rep, 0.0).astype(BF16)
    bias = bias_ref[:, 0:1]
    neg_upper2 = _neg_strict_upper2(tk)

    @pl.when(c == 0)
    def _():
        pad = jnp.zeros((128 - t_new, dsb), BF16)
        kb = jnp.concatenate([kn_ref[...].astype(BF16), pad], axis=0)
        vb = jnp.concatenate([vn_ref[...].astype(BF16), pad], axis=0)
        t_of_row = lax.broadcasted_iota(jnp.int32, (rows, 128), 0) % t_new
        key = lax.broadcasted_iota(jnp.int32, (rows, 128), 1)
        carry_ref[...], acc_ref[...] = _sb_step(
            qbd, kb, vb, bias, _neg_strict_upper2(128), jnp.zeros((rows, 128), F32),
            jnp.zeros((rows, dsb), F32), key < t_of_row, False)

    for cp in page_copies(b, c, slot):
        cp.wait()

    pages_per_block = tk // PAGE_SIZE
    n_blocks = pages_per_chunk // pages_per_block

    def block_of(buf, i):
        first = (n_blocks - 1 - i) * pages_per_block
        return jnp.concatenate([buf[slot, first + p] for p in range(pages_per_block)], axis=1).astype(BF16)

    groups = [range(g, min(g + SAMPLE_GROUP, n_blocks)) for g in range(0, n_blocks, SAMPLE_GROUP)]
    carry, acc = carry_ref[...], acc_ref[...]
    zs = [_sb_scores(qbd, block_of(kbuf, i), bias, True) for i in groups[0]]
    for gi, group in enumerate(groups):
        parts = []
        for z in zs:
            log_beta, spm, hilo = _sb_softplus(z, None)
            parts.append((log_beta, spm, _dot(hilo, neg_upper2)))
        if gi + 1 < len(groups):
            zs = [_sb_scores(qbd, block_of(kbuf, i), bias, True) for i in groups[gi + 1]]
        for i, (log_beta, spm, neg_later) in zip(group, parts):
            wb, carry = _sb_weights(log_beta, spm, neg_later, carry, None)
            acc = _sb_pv(wb, block_of(vbuf, i), acc, True)
    carry_ref[...] = carry
    acc_ref[...] = acc

    @pl.when(c == n_chunks - 1)
    def _():
        own = jnp.where(head_mask, acc, 0.0).reshape(nh, t_new, dsb)
        o_ref[...] = jnp.sum(own, axis=0)


def _sb_sample(q, k_new, v_new, cache_k, cache_v, page_table, bias_rows, *, t_new, n_chunks):
    m, dsb = q.shape
    nbatch = m // t_new
    n_pages = page_table.shape[1]
    ppc = n_pages // n_chunks
    rows = bias_rows.shape[0]
    kern = functools.partial(_sb_sample_kernel, n_chunks=n_chunks, pages_per_chunk=ppc, t_new=t_new)
    row = pl.BlockSpec((t_new, dsb), lambda b, c, pt: (b, 0))
    grid_spec = pltpu.PrefetchScalarGridSpec(
        num_scalar_prefetch=1,
        grid=(nbatch, n_chunks),
        in_specs=[pl.BlockSpec(bias_rows.shape, lambda b, c, pt: (0, 0)), row, row, row,
                  pl.BlockSpec(memory_space=pl.ANY), pl.BlockSpec(memory_space=pl.ANY)],
        out_specs=row,
        scratch_shapes=[pltpu.VMEM((2, ppc, dsb, PAGE_SIZE), F32),
                        pltpu.VMEM((2, ppc, dsb, PAGE_SIZE), F32),
                        pltpu.SemaphoreType.DMA((2, 2)),
                        pltpu.VMEM((rows, 128), F32),
                        pltpu.VMEM((rows, dsb), F32)])
    return pl.pallas_call(
        kern,
        grid_spec=grid_spec,
        out_shape=jax.ShapeDtypeStruct((m, dsb), F32),
        compiler_params=pltpu.CompilerParams(dimension_semantics=("arbitrary", "arbitrary"),
                                             vmem_limit_bytes=VMEM_LIMIT),
        name="sb_sample",
    )(page_table, bias_rows, q, k_new, v_new, cache_k, cache_v)


def _mix_xattn_kernel(x_ref, yc_ref, ysb_ref, wo_ref, wq_ref, wxo_ref, mk_ref, mv_ref, g_ref, b_ref,
                      o_ref, *, alpha, n_mem_batches):
    x = x_ref[...]
    dc = yc_ref.shape[-1]
    mix = (_dot(yc_ref[...].astype(BF16), wo_ref[0:dc, :])
           + _dot(ysb_ref[...].astype(BF16), wo_ref[dc:, :]))
    x2 = _layer_norm_rows(alpha * x + mix, g_ref[1:2, :], b_ref[1:2, :])
    q = _dot(x2.astype(BF16), wq_ref[...]).astype(BF16)
    tm, dx = q.shape
    rpb = tm // n_mem_batches
    n_heads = dx // X_HEAD_DIM
    outs = []
    for j in range(n_mem_batches):
        mk = mk_ref[j].astype(BF16)
        mv = mv_ref[j].astype(BF16)
        qj = q[j * rpb:(j + 1) * rpb, :]
        heads = []
        for h in range(n_heads):
            hs = slice(h * X_HEAD_DIM, (h + 1) * X_HEAD_DIM)
            s = _dot_nt(qj[:, hs], mk[:, hs]) * (X_HEAD_DIM ** -0.5)
            e = jnp.exp(s - jnp.max(s, axis=-1, keepdims=True))
            a = e / jnp.sum(e, axis=-1, keepdims=True)
            heads.append(_dot(a.astype(BF16), mv[:, hs]))
        outs.append(jnp.concatenate(heads, axis=-1))
    o = outs[0] if n_mem_batches == 1 else jnp.concatenate(outs, axis=0)
    xo = _dot(o.astype(BF16), wxo_ref[...])
    o_ref[...] = _layer_norm_rows(alpha * x2 + xo, g_ref[2:3, :], b_ref[2:3, :])


def _mix_xattn(x, yc, ysb, w_out, w_xq, w_xo, mk, mv, g, b, *, alpha, tm, rows_per_mem):
    m, d = x.shape
    nbm = tm // rows_per_mem
    n_mem, dx = mk.shape[1], mk.shape[2]
    kern = functools.partial(_mix_xattn_kernel, alpha=alpha, n_mem_batches=nbm)
    row = lambda c: pl.BlockSpec((tm, c), lambda i: (i, 0))
    if rows_per_mem >= tm:
        mem_spec = pl.BlockSpec((1, n_mem, dx), lambda i: (i * tm // rows_per_mem, 0, 0))
        nbm = 1
        kern = functools.partial(_mix_xattn_kernel, alpha=alpha, n_mem_batches=1)
    else:
        mem_spec = pl.BlockSpec((nbm, n_mem, dx), lambda i: (i, 0, 0))
    return pl.pallas_call(
        kern,
        grid=(m // tm,),
        in_specs=[row(d), row(yc.shape[1]), row(ysb.shape[1]),
                  _resident(w_out.shape), _resident(w_xq.shape), _resident(w_xo.shape),
                  mem_spec, mem_spec, _resident(g.shape), _resident(b.shape)],
        out_specs=row(d),
        out_shape=jax.ShapeDtypeStruct((m, d), F32),
        compiler_params=pltpu.CompilerParams(dimension_semantics=("arbitrary",),
                                             vmem_limit_bytes=VMEM_LIMIT),
        name="mix_xattn",
    )(x, yc, ysb, w_out, w_xq, w_xo, mk, mv, g, b)


def _tile(m, want):
    return want if m % want == 0 else m


def kernel(x_prompt, x_sample, mem_prompt, cache_k, cache_v, state_conv, cache_mem_k, cache_mem_v, page_table, ln_g, ln_b, w_ffn1_gu, w_ffn1_down, w_in, conv_w, sb_bias, w_out, w_xq, w_xk, w_xv, w_xo, w_ffn2_gu, w_ffn2_down):
    depth = ln_g.shape[0]
    alpha = (2 * depth) ** 0.25
    bp, s, d = x_prompt.shape
    db, t_new, _ = x_sample.shape
    n_mem = mem_prompt.shape[1]
    dc = conv_w.shape[-1]
    dsb = (w_in.shape[-1] - 3 * dc) // 3
    nh = dsb // SB_HEAD_DIM
    d_ff = w_ffn1_down.shape[1]
    ff_chunk = d_ff // 2 if (d_ff // 2) % 128 == 0 else d_ff
    page = cache_k.shape[2]

    yp = x_prompt.reshape(bp * s, d)
    ys = x_sample.reshape(db * t_new, d)
    outs = {n: [] for n in ("kp", "vp", "cp", "mkp", "mvp", "ks", "vs", "cs")}
    for l in range(depth):
        g, b = ln_g[l], ln_b[l]
        w1gu, w1dn = w_ffn1_gu[l].astype(BF16), w_ffn1_down[l].astype(BF16)
        w2gu, w2dn = w_ffn2_gu[l].astype(BF16), w_ffn2_down[l].astype(BF16)
        win, wout = w_in[l].astype(BF16), w_out[l].astype(BF16)
        wxq, wxk, wxv, wxo = (w[l].astype(BF16) for w in (w_xq, w_xk, w_xv, w_xo))
        cw = conv_w[l]
        bias = sb_bias[l].astype(F32)
        ffn = functools.partial(_ffn_ln, alpha=alpha, ff_chunk=ff_chunk)

        mk, mv = _memproj(mem_prompt.reshape(bp * n_mem, d), wxk, wxv)
        x1 = ffn(yp, w1gu, w1dn, g[0:1], b[0:1], tm=_tile(bp * s, 512))
        yc, q, kt, vt, kt_bf, vt_bf, cst = _inproj_prompt(
            x1.reshape(bp, s, d), win[:, :3 * dc + dsb], win[:, 3 * dc + dsb:].T, cw, tm=_tile(s, 512))
        ysb = _sb_prompt(q, kt_bf, vt_bf, bias, tq=_tile(s, SB_KEY_BLOCK), heads_per_step=min(nh, 4))
        tm = _tile(s, 512)
        x3 = _mix_xattn(x1, yc.reshape(bp * s, dc), ysb.reshape(bp * s, dsb), wout, wxq, wxo,
                        mk.reshape(bp, n_mem, -1), mv.reshape(bp, n_mem, -1), g, b,
                        alpha=alpha, tm=tm, rows_per_mem=s)
        yp = ffn(x3, w2gu, w2dn, g[3:4], b[3:4], tm=_tile(bp * s, 512))
        outs["kp"].append(kt.reshape(bp, nh, SB_HEAD_DIM, s).transpose(0, 3, 1, 2))
        outs["vp"].append(vt.reshape(bp, nh, SB_HEAD_DIM, s).transpose(0, 3, 1, 2))
        outs["cp"].append(cst)
        outs["mkp"].append(mk.reshape(bp, n_mem, -1, X_HEAD_DIM))
        outs["mvp"].append(mv.reshape(bp, n_mem, -1, X_HEAD_DIM))

        ms = db * t_new
        x1 = ffn(ys, w1gu, w1dn, g[0:1], b[0:1], tm=_tile(ms, 512))
        c0 = jnp.repeat(state_conv[l][:, 0, :], t_new, axis=0)
        c1 = jnp.repeat(state_conv[l][:, 1, :], t_new, axis=0)
        yc, q, k, v, u = _inproj_sample(x1, win, cw, c0, c1, seq_rows=t_new, tm=_tile(ms, 512))
        bias_rows = jnp.broadcast_to(jnp.repeat(bias, t_new)[:, None], (nh * t_new, 128))
        ck_t = cache_k[l].transpose(0, 2, 3, 1).reshape(-1, dsb, page)
        cv_t = cache_v[l].transpose(0, 2, 3, 1).reshape(-1, dsb, page)
        ysb = _sb_sample(q, k, v, ck_t, cv_t, page_table, bias_rows, t_new=t_new, n_chunks=2)
        mem_per_step = 8
        x3 = _mix_xattn(x1, yc, ysb, wout, wxq, wxo,
                        cache_mem_k[l].reshape(db, n_mem, -1), cache_mem_v[l].reshape(db, n_mem, -1),
                        g, b, alpha=alpha, tm=_tile(ms, mem_per_step * t_new), rows_per_mem=t_new)
        ys = ffn(x3, w2gu, w2dn, g[3:4], b[3:4], tm=_tile(ms, 512))
        outs["ks"].append(k.reshape(db, t_new, nh, SB_HEAD_DIM))
        outs["vs"].append(v.reshape(db, t_new, nh, SB_HEAD_DIM))
        outs["cs"].append(u.reshape(db, t_new, dc)[:, t_new - (CONV_WIDTH - 1):, :])

    st = lambda n: jnp.stack(outs[n])
    return (yp.reshape(bp, s, d), ys.reshape(db, t_new, d), st("kp"), st("vp"), st("cp"), st("mkp"),
            st("mvp"), st("ks"), st("vs"), st("cs"))
```

```python
import functools

import jax
import jax.numpy as jnp
from jax import lax
from jax.experimental import pallas as pl
from jax.experimental.pallas import tpu as pltpu

F32 = jnp.float32
BF16 = jnp.bfloat16

LN_EPS = 1e-5
NEG_LOG2_E = -1.4426950408889634
SB_HEAD_DIM = 64
X_HEAD_DIM = 256
CONV_WIDTH = 3
PAGE_SIZE = 128
SB_KEY_BLOCK = 256
SAMPLE_GROUP = 4
VMEM_LIMIT = 56 * 1024 * 1024


def _dot(a, b):
    return jnp.dot(a, b, preferred_element_type=F32)


def _dot_nt(a, b):
    return lax.dot_general(a, b, (((1,), (1,)), ((), ())), preferred_element_type=F32)


def _layer_norm_rows(y, g, b):
    mu = jnp.mean(y, axis=-1, keepdims=True)
    d = y - mu
    var = jnp.mean(d * d, axis=-1, keepdims=True)
    return d * lax.rsqrt(var + LN_EPS) * g + b


def _resident(shape):
    nd = len(shape)
    return pl.BlockSpec(shape, lambda *_: (0,) * nd, pipeline_mode=pl.Buffered(1))


def _ffn_ln_kernel(x_ref, wgu_ref, wdn_ref, g_ref, b_ref, o_ref, *, d_ff, ff_chunk, alpha):
    x = x_ref[...]
    xb = x.astype(BF16)
    acc = jnp.zeros(x.shape, F32)
    for c in range(d_ff // ff_chunk):
        lo = c * ff_chunk
        gate = _dot(xb, wgu_ref[:, lo:lo + ff_chunk])
        up = _dot(xb, wgu_ref[:, d_ff + lo:d_ff + lo + ff_chunk])
        h = (gate / (1.0 + jnp.exp(-gate))) * up
        acc = acc + _dot(h.astype(BF16), wdn_ref[lo:lo + ff_chunk, :])
    y = alpha * x + 0.5 * acc
    o_ref[...] = _layer_norm_rows(y, g_ref[...], b_ref[...])


def _ffn_ln(x, wgu, wdn, g, b, *, alpha, tm, ff_chunk):
    m, d = x.shape
    d_ff = wdn.shape[0]
    kern = functools.partial(_ffn_ln_kernel, d_ff=d_ff, ff_chunk=ff_chunk, alpha=alpha)
    return pl.pallas_call(
        kern,
        grid=(m // tm,),
        in_specs=[pl.BlockSpec((tm, d), lambda i: (i, 0)),
                  _resident(wgu.shape), _resident(wdn.shape),
                  _resident(g.shape), _resident(b.shape)],
        out_specs=pl.BlockSpec((tm, d), lambda i: (i, 0)),
        out_shape=jax.ShapeDtypeStruct((m, d), F32),
        compiler_params=pltpu.CompilerParams(dimension_semantics=("arbitrary",),
                                             vmem_limit_bytes=VMEM_LIMIT),
        name="ffn_ln",
    )(x, wgu, wdn, g, b)


def _memproj_kernel(x_ref, wk_ref, wv_ref, k_ref, v_ref):
    xb = x_ref[...].astype(BF16)
    k_ref[...] = _dot(xb, wk_ref[...])
    v_ref[...] = _dot(xb, wv_ref[...])


def _memproj(x, wk, wv):
    m, d = x.shape
    n = wk.shape[1]
    return pl.pallas_call(
        _memproj_kernel,
        grid=(1,),
        in_specs=[_resident(x.shape), _resident(wk.shape), _resident(wv.shape)],
        out_specs=[pl.BlockSpec((m, n), lambda i: (0, 0)), pl.BlockSpec((m, n), lambda i: (0, 0))],
        out_shape=[jax.ShapeDtypeStruct((m, n), F32), jax.ShapeDtypeStruct((m, n), F32)],
        compiler_params=pltpu.CompilerParams(vmem_limit_bytes=VMEM_LIMIT),
        name="memproj",
    )(x, wk, wv)


def _conv_taps(u, prev2, prev1, cw_ref, seq_rows):
    r = u.shape[0]
    row = lax.broadcasted_iota(jnp.int32, (r, 1), 0)
    t = row if seq_rows is None else row % seq_rows
    r1 = pltpu.roll(u, 1, axis=0)
    r2 = pltpu.roll(u, 2, axis=0)
    back1 = jnp.where(t >= 1, r1, prev1)
    back2 = jnp.where(t >= 2, r2, jnp.where(t == 1, prev1, prev2))
    return cw_ref[0:1, :] * back2 + cw_ref[1:2, :] * back1 + cw_ref[2:3, :] * u


def _inproj_prompt_kernel(x_ref, w_ref, wkvt_ref, cw_ref, yc_ref, q_ref, kt_ref, vt_ref, ktb_ref, vtb_ref,
                          st_ref, carry_ref, *, dc, dsb):
    i = pl.program_id(1)

    @pl.when(i == 0)
    def _():
        carry_ref[...] = jnp.zeros(carry_ref.shape, F32)

    xb = x_ref[...].astype(BF16)
    p = _dot(xb, w_ref[...])
    gb, gc, hx = p[:, 0:dc], p[:, dc:2 * dc], p[:, 2 * dc:3 * dc]
    q = p[:, 3 * dc:3 * dc + dsb]
    u = gc * hx
    prev = carry_ref[...]
    z = _conv_taps(u, prev[6:7, :], prev[7:8, :], cw_ref, None)
    yc_ref[...] = (gb * z).astype(yc_ref.dtype)
    tail = u[u.shape[0] - 8:, :]
    carry_ref[...] = tail
    st_ref[...] = tail[6:8, :]
    q_ref[...] = (q * (SB_HEAD_DIM ** -0.5)).astype(q_ref.dtype)
    kvt = _dot_nt(wkvt_ref[...], xb)
    kt, vt = kvt[0:dsb, :], kvt[dsb:, :]
    kt_ref[...] = kt
    vt_ref[...] = vt
    ktb_ref[...] = kt.astype(BF16)
    vtb_ref[...] = vt.astype(BF16)


def _inproj_prompt(x, w_main, w_kv_t, conv_w, *, tm):
    bsz, s, d = x.shape
    dc = conv_w.shape[1]
    dsb = w_kv_t.shape[0] // 2
    kern = functools.partial(_inproj_prompt_kernel, dc=dc, dsb=dsb)
    row = lambda c: pl.BlockSpec((None, tm, c), lambda b, i: (b, i, 0))
    colm = pl.BlockSpec((None, dsb, tm), lambda b, i: (b, 0, i))
    return pl.pallas_call(
        kern,
        grid=(bsz, s // tm),
        in_specs=[row(d), _resident(w_main.shape), _resident(w_kv_t.shape), _resident(conv_w.shape)],
        out_specs=[row(dc), row(dsb), colm, colm, colm, colm,
                   pl.BlockSpec((None, CONV_WIDTH - 1, dc), lambda b, i: (b, 0, 0))],
        out_shape=[jax.ShapeDtypeStruct((bsz, s, dc), BF16),
                   jax.ShapeDtypeStruct((bsz, s, dsb), BF16),
                   jax.ShapeDtypeStruct((bsz, dsb, s), F32),
                   jax.ShapeDtypeStruct((bsz, dsb, s), F32),
                   jax.ShapeDtypeStruct((bsz, dsb, s), BF16),
                   jax.ShapeDtypeStruct((bsz, dsb, s), BF16),
                   jax.ShapeDtypeStruct((bsz, CONV_WIDTH - 1, dc), F32)],
        scratch_shapes=[pltpu.VMEM((8, dc), F32)],
        compiler_params=pltpu.CompilerParams(dimension_semantics=("arbitrary", "arbitrary"),
                                             vmem_limit_bytes=VMEM_LIMIT),
        name="inproj_prompt",
    )(x, w_main, w_kv_t, conv_w)


def _inproj_sample_kernel(x_ref, w_ref, cw_ref, c0_ref, c1_ref, yc_ref, q_ref, k_ref, v_ref, u_ref,
                          *, dc, dsb, seq_rows):
    p = _dot(x_ref[...].astype(BF16), w_ref[...])
    gb, gc, hx = p[:, 0:dc], p[:, dc:2 * dc], p[:, 2 * dc:3 * dc]
    o = 3 * dc
    u = gc * hx
    z = _conv_taps(u, c0_ref[...], c1_ref[...], cw_ref, seq_rows)
    yc_ref[...] = (gb * z).astype(yc_ref.dtype)
    u_ref[...] = u
    q_ref[...] = p[:, o:o + dsb] * (SB_HEAD_DIM ** -0.5)
    k_ref[...] = p[:, o + dsb:o + 2 * dsb]
    v_ref[...] = p[:, o + 2 * dsb:o + 3 * dsb]


def _inproj_sample(x, w_in, conv_w, c0, c1, *, seq_rows, tm):
    m, d = x.shape
    dc = conv_w.shape[1]
    dsb = (w_in.shape[1] - 3 * dc) // 3
    kern = functools.partial(_inproj_sample_kernel, dc=dc, dsb=dsb, seq_rows=seq_rows)
    row = lambda c: pl.BlockSpec((tm, c), lambda i: (i, 0))
    return pl.pallas_call(
        kern,
        grid=(m // tm,),
        in_specs=[row(d), _resident(w_in.shape), _resident(conv_w.shape), row(dc), row(dc)],
        out_specs=[row(dc), row(dsb), row(dsb), row(dsb), row(dc)],
        out_shape=[jax.ShapeDtypeStruct((m, dc), BF16),
                   jax.ShapeDtypeStruct((m, dsb), F32),
                   jax.ShapeDtypeStruct((m, dsb), F32),
                   jax.ShapeDtypeStruct((m, dsb), F32),
                   jax.ShapeDtypeStruct((m, dc), F32)],
        compiler_params=pltpu.CompilerParams(dimension_semantics=("arbitrary",),
                                             vmem_limit_bytes=VMEM_LIMIT),
        name="inproj_sample",
    )(x, w_in, conv_w, c0, c1)


def _neg_strict_upper(n, copies):
    r = lax.broadcasted_iota(jnp.int32, (copies * n, n), 0)
    c = lax.broadcasted_iota(jnp.int32, (copies * n, n), 1)
    for _ in range(copies - 1):
        r = jnp.where(r >= n, r - n, r)
    return jnp.where(r > c, -1.0, 0.0).astype(BF16)


def _neg_strict_upper2(n):
    return _neg_strict_upper(n, 2)


def _sb_scores(qb, kb, bias, transposed):
    return (_dot(qb, kb) if transposed else _dot_nt(qb, kb)) + bias


def _sb_softplus(z, valid, split):
    sp = jnp.maximum(z, 0.0) + jnp.log(1.0 + jnp.exp2(jnp.abs(z) * NEG_LOG2_E))
    spm = sp if valid is None else jnp.where(valid, sp, 0.0)
    hi = spm.astype(BF16)
    if not split:
        return z - sp, spm, hi
    lo = (spm - hi.astype(F32)).astype(BF16)
    return z - sp, spm, jnp.concatenate([hi, lo], axis=1)


def _sb_weights(log_beta, spm, neg_later, carry, valid):
    w = jnp.exp(log_beta + neg_later + jnp.tile(carry, (1, log_beta.shape[1] // carry.shape[1])))
    if valid is not None:
        w = jnp.where(valid, w, 0.0)
    return w.astype(BF16), carry - jnp.sum(spm, axis=-1, keepdims=True)


def _sb_pv(wb, vb, acc, transposed):
    return acc + (_dot_nt(wb, vb) if transposed else _dot(wb, vb))


def _sb_step(qb, kb, vb, bias, neg_upper2, carry, acc, valid, transposed):
    log_beta, spm, hilo = _sb_softplus(_sb_scores(qb, kb, bias, transposed), valid, True)
    wb, carry = _sb_weights(log_beta, spm, _dot(hilo, neg_upper2), carry, valid)
    return carry, _sb_pv(wb, vb, acc, transposed)


def _sb_prompt_kernel(bias_ref, q_ref, k_ref, v_ref, o_ref, z_scr, w_scr, carry_scr, acc_scr, *, tq, n_heads):
    tk = SB_KEY_BLOCK
    hg = pl.program_id(1)
    qi = pl.program_id(2)
    neg_upper = _neg_strict_upper(tk, 1)
    qall = q_ref[...]
    heads = tuple(range(n_heads))
    row = lax.broadcasted_iota(jnp.int32, (tq, tk), 0)
    col = lax.broadcasted_iota(jnp.int32, (tq, tk), 1)

    part_col = lax.broadcasted_iota(jnp.int32, (tq, SB_HEAD_DIM), 1)
    ones_rows = jnp.where(lax.broadcasted_iota(jnp.int32, (SB_HEAD_DIM, tk), 0) < 3, 1.0, 0.0).astype(BF16)
    qs = []
    for e in heads:
        rest = jnp.full((tq, SB_HEAD_DIM), bias_ref[n_heads * hg + e], F32)
        cols = jnp.zeros((tq, SB_HEAD_DIM), F32)
        for p in range(3):
            part = rest.astype(BF16).astype(F32)
            cols = jnp.where(part_col == p, part, cols)
            rest = rest - part
        qs.append(jnp.concatenate([qall[:, e * SB_HEAD_DIM:(e + 1) * SB_HEAD_DIM], cols.astype(BF16)], axis=1))

    def head_rows(ref, e, j):
        return ref[e * SB_HEAD_DIM:(e + 1) * SB_HEAD_DIM, pl.ds(pl.multiple_of(j * tk, tk), tk)]

    def scores(e, j):
        return _dot(qs[e], jnp.concatenate([head_rows(k_ref, e, j), ones_rows], axis=0))

    def middle(valid, j_next):
        parts = []
        for e in heads:
            log_beta, spm, sp_bf = _sb_softplus(z_scr[e], valid, False)
            parts.append((log_beta, spm, _dot(sp_bf, neg_upper)))
            z_scr[e] = scores(e, j_next)
        for e, (log_beta, spm, neg_later) in zip(heads, parts):
            w_scr[e], carry_scr[e] = _sb_weights(log_beta, spm, neg_later, carry_scr[e], valid)

    def add_values(j):
        for e in heads:
            acc_scr[e] = _sb_pv(w_scr[e], head_rows(v_ref, e, j), acc_scr[e], True)

    for e in heads:
        z_scr[e] = scores(e, qi)
    carry_scr[...] = jnp.zeros(carry_scr.shape, F32)
    acc_scr[...] = jnp.zeros(acc_scr.shape, F32)
    middle(col < row, jnp.maximum(qi - 1, 0))

    def body(jj, _):
        j = qi - 1 - jj
        add_values(j + 1)
        middle(None, jnp.maximum(j - 1, 0))
        return 0

    lax.fori_loop(0, qi, body, 0)
    add_values(0)
    o_ref[...] = jnp.concatenate([acc_scr[e] for e in heads], axis=-1).astype(o_ref.dtype)


def _sb_prompt(q, k_t, v_t, bias, *, tq, heads_per_step):
    bsz, s, dsb = q.shape
    nh = dsb // SB_HEAD_DIM
    assert tq == SB_KEY_BLOCK and s % tq == 0 and nh % heads_per_step == 0
    gw = heads_per_step * SB_HEAD_DIM
    kern = functools.partial(_sb_prompt_kernel, tq=tq, n_heads=heads_per_step)
    kv_spec = pl.BlockSpec((None, gw, s), lambda b, hg, qi: (b, hg, 0))
    return pl.pallas_call(
        kern,
        grid=(bsz, nh // heads_per_step, s // tq),
        in_specs=[pl.BlockSpec(memory_space=pltpu.SMEM),
                  pl.BlockSpec((None, tq, gw), lambda b, hg, qi: (b, qi, hg)),
                  kv_spec, kv_spec],
        out_specs=pl.BlockSpec((None, tq, gw), lambda b, hg, qi: (b, qi, hg)),
        out_shape=jax.ShapeDtypeStruct((bsz, s, dsb), BF16),
        scratch_shapes=[pltpu.VMEM((heads_per_step, tq, SB_KEY_BLOCK), F32),
                        pltpu.VMEM((heads_per_step, tq, SB_KEY_BLOCK), BF16),
                        pltpu.VMEM((heads_per_step, tq, 128), F32),
                        pltpu.VMEM((heads_per_step, tq, SB_HEAD_DIM), F32)],
        compiler_params=pltpu.CompilerParams(
            dimension_semantics=("arbitrary", "arbitrary", "arbitrary"),
            vmem_limit_bytes=VMEM_LIMIT),
        name="sb_prompt",
    )(bias, q, k_t, v_t)


def _sb_sample_kernel(pt_ref, bias_ref, q_ref, kn_ref, vn_ref, ck_ref, cv_ref, o_ref,
                      kbuf, vbuf, sem, carry_ref, acc_ref, *, n_chunks, pages_per_chunk, t_new):
    tk = SB_KEY_BLOCK
    b = pl.program_id(0)
    c = pl.program_id(1)
    nb = pl.num_programs(0)
    step = b * n_chunks + c
    slot = step % 2
    dsb = q_ref.shape[-1]
    nh = dsb // SB_HEAD_DIM
    rows = nh * t_new

    def page_copies(bb, cc, sl):
        first = (n_chunks - 1 - cc) * pages_per_chunk
        cps = []
        for p in range(pages_per_chunk):
            page = pt_ref[bb, first + p]
            cps.append(pltpu.make_async_copy(ck_ref.at[page], kbuf.at[sl, p], sem.at[sl, 0]))
            cps.append(pltpu.make_async_copy(cv_ref.at[page], vbuf.at[sl, p], sem.at[sl, 1]))
        return cps

    @pl.when(step == 0)
    def _():
        for cp in page_copies(b, c, slot):
            cp.start()

    @pl.when(step + 1 < nb * n_chunks)
    def _():
        nxt = step + 1
        for cp in page_copies(nxt // n_chunks, nxt % n_chunks, 1 - slot):
            cp.start()

    rrow = lax.broadcasted_iota(jnp.int32, (rows, dsb), 0)
    rcol = lax.broadcasted_iota(jnp.int32, (rows, dsb), 1)
    head_mask = (rrow // t_new) == (rcol // SB_HEAD_DIM)
    q_rep = jnp.concatenate([q_ref[...]] * nh, axis=0)
    qbd = jnp.where(head_mask, q_rep, 0.0).astype(BF16)
    bias = bias_ref[:, 0:1]
    neg_upper2 = _neg_strict_upper2(tk)

    @pl.when(c == 0)
    def _():
        pad = jnp.zeros((128 - t_new, dsb), BF16)
        kb = jnp.concatenate([kn_ref[...].astype(BF16), pad], axis=0)
        vb = jnp.concatenate([vn_ref[...].astype(BF16), pad], axis=0)
        t_of_row = lax.broadcasted_iota(jnp.int32, (rows, 128), 0) % t_new
        key = lax.broadcasted_iota(jnp.int32, (rows, 128), 1)
        carry_ref[...], acc_ref[...] = _sb_step(
            qbd, kb, vb, bias, _neg_strict_upper2(128), jnp.zeros((rows, 128), F32),
            jnp.zeros((rows, dsb), F32), key < t_of_row, False)

    for cp in page_copies(b, c, slot):
        cp.wait()

    pages_per_block = tk // PAGE_SIZE
    n_blocks = pages_per_chunk // pages_per_block

    def block_of(buf, i):
        first = (n_blocks - 1 - i) * pages_per_block
        return jnp.concatenate([buf[slot, first + p] for p in range(pages_per_block)], axis=1).astype(BF16)

    groups = [range(g, min(g + SAMPLE_GROUP, n_blocks)) for g in range(0, n_blocks, SAMPLE_GROUP)]
    carry, acc = carry_ref[...], acc_ref[...]
    zs = [_sb_scores(qbd, block_of(kbuf, i), bias, True) for i in groups[0]]
    for gi, group in enumerate(groups):
        parts = []
        for z in zs:
            log_beta, spm, hilo = _sb_softplus(z, None, True)
            parts.append((log_beta, spm, _dot(hilo, neg_upper2)))
        if gi + 1 < len(groups):
            zs = [_sb_scores(qbd, block_of(kbuf, i), bias, True) for i in groups[gi + 1]]
        for i, (log_beta, spm, neg_later) in zip(group, parts):
            wb, carry = _sb_weights(log_beta, spm, neg_later, carry, None)
            acc = _sb_pv(wb, block_of(vbuf, i), acc, True)
    carry_ref[...] = carry
    acc_ref[...] = acc

    @pl.when(c == n_chunks - 1)
    def _():
        own = jnp.where(head_mask, acc, 0.0).reshape(nh, t_new, dsb)
        o_ref[...] = jnp.sum(own, axis=0)


def _sb_sample(q, k_new, v_new, cache_k, cache_v, page_table, bias_rows, *, t_new, n_chunks):
    m, dsb = q.shape
    nbatch = m // t_new
    n_pages = page_table.shape[1]
    ppc = n_pages // n_chunks
    rows = bias_rows.shape[0]
    kern = functools.partial(_sb_sample_kernel, n_chunks=n_chunks, pages_per_chunk=ppc, t_new=t_new)
    row = pl.BlockSpec((t_new, dsb), lambda b, c, pt: (b, 0))
    grid_spec = pltpu.PrefetchScalarGridSpec(
        num_scalar_prefetch=1,
        grid=(nbatch, n_chunks),
        in_specs=[pl.BlockSpec(bias_rows.shape, lambda b, c, pt: (0, 0)), row, row, row,
                  pl.BlockSpec(memory_space=pl.ANY), pl.BlockSpec(memory_space=pl.ANY)],
        out_specs=row,
        scratch_shapes=[pltpu.VMEM((2, ppc, dsb, PAGE_SIZE), F32),
                        pltpu.VMEM((2, ppc, dsb, PAGE_SIZE), F32),
                        pltpu.SemaphoreType.DMA((2, 2)),
                        pltpu.VMEM((rows, 128), F32),
                        pltpu.VMEM((rows, dsb), F32)])
    return pl.pallas_call(
        kern,
        grid_spec=grid_spec,
        out_shape=jax.ShapeDtypeStruct((m, dsb), F32),
        compiler_params=pltpu.CompilerParams(dimension_semantics=("arbitrary", "arbitrary"),
                                             vmem_limit_bytes=VMEM_LIMIT),
        name="sb_sample",
    )(page_table, bias_rows, q, k_new, v_new, cache_k, cache_v)


def _mix_xattn_kernel(x_ref, yc_ref, ysb_ref, wo_ref, wq_ref, wxo_ref, mk_ref, mv_ref, g_ref, b_ref,
                      o_ref, *, alpha, n_mem_batches):
    x = x_ref[...]
    dc = yc_ref.shape[-1]
    mix = (_dot(yc_ref[...].astype(BF16), wo_ref[0:dc, :])
           + _dot(ysb_ref[...].astype(BF16), wo_ref[dc:, :]))
    x2 = _layer_norm_rows(alpha * x + mix, g_ref[1:2, :], b_ref[1:2, :])
    q = _dot(x2.astype(BF16), wq_ref[...]).astype(BF16)
    tm, dx = q.shape
    rpb = tm // n_mem_batches
    n_heads = dx // X_HEAD_DIM
    lane_tiles = X_HEAD_DIM // 128

    def mem_head(ref, j, h):
        if ref.shape[-1] == dx:
            return ref[j, :, h * X_HEAD_DIM:(h + 1) * X_HEAD_DIM].astype(BF16)
        per_tok = lane_tiles * n_heads
        parts = [ref[j, pl.ds(t * n_heads + h, ref.shape[1] // per_tok, stride=per_tok), :]
                 for t in range(lane_tiles)]
        return jnp.concatenate(parts, axis=1).astype(BF16)

    pairs = [(j, h) for j in range(n_mem_batches) for h in range(n_heads)]
    scores = [_dot_nt(q[j * rpb:(j + 1) * rpb, h * X_HEAD_DIM:(h + 1) * X_HEAD_DIM], mem_head(mk_ref, j, h))
              * (X_HEAD_DIM ** -0.5) for j, h in pairs]
    probs = []
    for s in scores:
        e = jnp.exp(s - jnp.max(s, axis=-1, keepdims=True))
        probs.append((e / jnp.sum(e, axis=-1, keepdims=True)).astype(BF16))
    outs = [_dot(a, mem_head(mv_ref, j, h)) for a, (j, h) in zip(probs, pairs)]
    rows = [jnp.concatenate(outs[j * n_heads:(j + 1) * n_heads], axis=-1) for j in range(n_mem_batches)]
    o = rows[0] if n_mem_batches == 1 else jnp.concatenate(rows, axis=0)
    xo = _dot(o.astype(BF16), wxo_ref[...])
    o_ref[...] = _layer_norm_rows(alpha * x2 + xo, g_ref[2:3, :], b_ref[2:3, :])


def _mix_xattn(x, yc, ysb, w_out, w_xq, w_xo, mk, mv, g, b, *, alpha, tm, rows_per_mem):
    m, d = x.shape
    nbm = max(tm // rows_per_mem, 1)
    kern = functools.partial(_mix_xattn_kernel, alpha=alpha, n_mem_batches=nbm)
    row = lambda c: pl.BlockSpec((tm, c), lambda i: (i, 0))
    if rows_per_mem >= tm:
        mem_spec = pl.BlockSpec((1,) + mk.shape[1:], lambda i: (i * tm // rows_per_mem, 0, 0))
    else:
        mem_spec = pl.BlockSpec((nbm,) + mk.shape[1:], lambda i: (i, 0, 0))
    return pl.pallas_call(
        kern,
        grid=(m // tm,),
        in_specs=[row(d), row(yc.shape[1]), row(ysb.shape[1]),
                  _resident(w_out.shape), _resident(w_xq.shape), _resident(w_xo.shape),
                  mem_spec, mem_spec, _resident(g.shape), _resident(b.shape)],
        out_specs=row(d),
        out_shape=jax.ShapeDtypeStruct((m, d), F32),
        compiler_params=pltpu.CompilerParams(dimension_semantics=("arbitrary",),
                                             vmem_limit_bytes=VMEM_LIMIT),
        name="mix_xattn",
    )(x, yc, ysb, w_out, w_xq, w_xo, mk, mv, g, b)


def _tile(m, want):
    return want if m % want == 0 else m


def kernel(x_prompt, x_sample, mem_prompt, cache_k, cache_v, state_conv, cache_mem_k, cache_mem_v, page_table, ln_g, ln_b, w_ffn1_gu, w_ffn1_down, w_in, conv_w, sb_bias, w_out, w_xq, w_xk, w_xv, w_xo, w_ffn2_gu, w_ffn2_down):
    depth = ln_g.shape[0]
    alpha = (2 * depth) ** 0.25
    bp, s, d = x_prompt.shape
    db, t_new, _ = x_sample.shape
    n_mem = mem_prompt.shape[1]
    dc = conv_w.shape[-1]
    dsb = (w_in.shape[-1] - 3 * dc) // 3
    nh = dsb // SB_HEAD_DIM
    d_ff = w_ffn1_down.shape[1]
    ff_chunk = d_ff // 2 if (d_ff // 2) % 128 == 0 else d_ff
    page = cache_k.shape[2]

    yp = x_prompt.reshape(bp * s, d)
    ys = x_sample.reshape(db * t_new, d)
    outs = {n: [] for n in ("kp", "vp", "cp", "mkp", "mvp", "ks", "vs", "cs")}
    for l in range(depth):
        g, b = ln_g[l], ln_b[l]
        w1gu, w1dn = w_ffn1_gu[l].astype(BF16), w_ffn1_down[l].astype(BF16)
        w2gu, w2dn = w_ffn2_gu[l].astype(BF16), w_ffn2_down[l].astype(BF16)
        win, wout = w_in[l].astype(BF16), w_out[l].astype(BF16)
        wxq, wxk, wxv, wxo = (w[l].astype(BF16) for w in (w_xq, w_xk, w_xv, w_xo))
        cw = conv_w[l]
        bias = sb_bias[l].astype(F32)
        ffn = functools.partial(_ffn_ln, alpha=alpha, ff_chunk=ff_chunk)

        mk, mv = _memproj(mem_prompt.reshape(bp * n_mem, d), wxk, wxv)
        x1 = ffn(yp, w1gu, w1dn, g[0:1], b[0:1], tm=_tile(bp * s, 512))
        yc, q, kt, vt, kt_bf, vt_bf, cst = _inproj_prompt(
            x1.reshape(bp, s, d), win[:, :3 * dc + dsb], win[:, 3 * dc + dsb:].T, cw, tm=_tile(s, 512))
        ysb = _sb_prompt(q, kt_bf, vt_bf, bias, tq=_tile(s, SB_KEY_BLOCK), heads_per_step=min(nh, 4))
        tm = _tile(s, 512)
        x3 = _mix_xattn(x1, yc.reshape(bp * s, dc), ysb.reshape(bp * s, dsb), wout, wxq, wxo,
                        mk.reshape(bp, n_mem, -1), mv.reshape(bp, n_mem, -1), g, b,
                        alpha=alpha, tm=tm, rows_per_mem=s)
        yp = ffn(x3, w2gu, w2dn, g[3:4], b[3:4], tm=_tile(bp * s, 512))
        outs["kp"].append(kt.reshape(bp, nh, SB_HEAD_DIM, s).transpose(0, 3, 1, 2))
        outs["vp"].append(vt.reshape(bp, nh, SB_HEAD_DIM, s).transpose(0, 3, 1, 2))
        outs["cp"].append(cst)
        outs["mkp"].append(mk.reshape(bp, n_mem, -1, X_HEAD_DIM))
        outs["mvp"].append(mv.reshape(bp, n_mem, -1, X_HEAD_DIM))

        ms = db * t_new
        x1 = ffn(ys, w1gu, w1dn, g[0:1], b[0:1], tm=_tile(ms, 512))
        c0 = jnp.repeat(state_conv[l][:, 0, :], t_new, axis=0)
        c1 = jnp.repeat(state_conv[l][:, 1, :], t_new, axis=0)
        yc, q, k, v, u = _inproj_sample(x1, win, cw, c0, c1, seq_rows=t_new, tm=_tile(ms, 512))
        bias_rows = jnp.broadcast_to(jnp.repeat(bias, t_new)[:, None], (nh * t_new, 128))
        ck_t = cache_k[l].transpose(0, 2, 3, 1).reshape(-1, dsb, page)
        cv_t = cache_v[l].transpose(0, 2, 3, 1).reshape(-1, dsb, page)
        ysb = _sb_sample(q, k, v, ck_t, cv_t, page_table, bias_rows, t_new=t_new, n_chunks=2)
        mem_per_step = 8

        def mem_rows(c):
            nxh = c.shape[2]
            c = c.reshape(db, n_mem, nxh, X_HEAD_DIM // 128, 128).transpose(0, 1, 3, 2, 4)
            return c.reshape(db, n_mem * nxh * (X_HEAD_DIM // 128), 128)

        x3 = _mix_xattn(x1, yc, ysb, wout, wxq, wxo, mem_rows(cache_mem_k[l]), mem_rows(cache_mem_v[l]),
                        g, b, alpha=alpha, tm=_tile(ms, mem_per_step * t_new), rows_per_mem=t_new)
        ys = ffn(x3, w2gu, w2dn, g[3:4], b[3:4], tm=_tile(ms, 512))
        outs["ks"].append(k.reshape(db, t_new, nh, SB_HEAD_DIM))
        outs["vs"].append(v.reshape(db, t_new, nh, SB_HEAD_DIM))
        outs["cs"].append(u.reshape(db, t_new, dc)[:, t_new - (CONV_WIDTH - 1):, :])

    st = lambda n: jnp.stack(outs[n])
    return (yp.reshape(bp, s, d), ys.reshape(db, t_new, d), st("kp"), st("vp"), st("cp"), st("mkp"),
            st("mvp"), st("ks"), st("vs"), st("cs"))
```

```python
import functools

import jax
import jax.numpy as jnp
from jax import lax
from jax.experimental import pallas as pl
from jax.experimental.pallas import tpu as pltpu

F32 = jnp.float32
BF16 = jnp.bfloat16

LN_EPS = 1e-5
NEG_LOG2_E = -1.4426950408889634
SB_HEAD_DIM = 64
X_HEAD_DIM = 256
CONV_WIDTH = 3
PAGE_SIZE = 128
SB_KEY_BLOCK = 256
SAMPLE_GROUP = 4
SAMPLE_CHUNKS = 4
SAMPLE_SLOTS = 4
VMEM_LIMIT = 56 * 1024 * 1024


def _dot(a, b):
    return jnp.dot(a, b, preferred_element_type=F32)


def _dot_nt(a, b):
    return lax.dot_general(a, b, (((1,), (1,)), ((), ())), preferred_element_type=F32)


def _layer_norm_rows(y, g, b):
    mu = jnp.mean(y, axis=-1, keepdims=True)
    d = y - mu
    var = jnp.mean(d * d, axis=-1, keepdims=True)
    return d * lax.rsqrt(var + LN_EPS) * g + b


def _resident(shape):
    nd = len(shape)
    return pl.BlockSpec(shape, lambda *_: (0,) * nd, pipeline_mode=pl.Buffered(1))


def _ffn_ln_kernel(x_ref, wgu_ref, wdn_ref, g_ref, b_ref, o_ref, *, d_ff, ff_chunk, alpha):
    x = x_ref[...]
    xb = x.astype(BF16)
    acc = jnp.zeros(x.shape, F32)
    for c in range(d_ff // ff_chunk):
        lo = c * ff_chunk
        gate = _dot(xb, wgu_ref[:, lo:lo + ff_chunk])
        up = _dot(xb, wgu_ref[:, d_ff + lo:d_ff + lo + ff_chunk])
        h = (gate / (1.0 + jnp.exp(-gate))) * up
        acc = acc + _dot(h.astype(BF16), wdn_ref[lo:lo + ff_chunk, :])
    y = alpha * x + 0.5 * acc
    o_ref[...] = _layer_norm_rows(y, g_ref[...], b_ref[...])


def _ffn_ln(x, wgu, wdn, g, b, *, alpha, tm, ff_chunk):
    m, d = x.shape
    d_ff = wdn.shape[0]
    kern = functools.partial(_ffn_ln_kernel, d_ff=d_ff, ff_chunk=ff_chunk, alpha=alpha)
    return pl.pallas_call(
        kern,
        grid=(m // tm,),
        in_specs=[pl.BlockSpec((tm, d), lambda i: (i, 0)),
                  _resident(wgu.shape), _resident(wdn.shape),
                  _resident(g.shape), _resident(b.shape)],
        out_specs=pl.BlockSpec((tm, d), lambda i: (i, 0)),
        out_shape=jax.ShapeDtypeStruct((m, d), F32),
        compiler_params=pltpu.CompilerParams(dimension_semantics=("arbitrary",),
                                             vmem_limit_bytes=VMEM_LIMIT),
        name="ffn_ln",
    )(x, wgu, wdn, g, b)


def _memproj_kernel(x_ref, wk_ref, wv_ref, k_ref, v_ref):
    xb = x_ref[...].astype(BF16)
    k_ref[...] = _dot(xb, wk_ref[...])
    v_ref[...] = _dot(xb, wv_ref[...])


def _memproj(x, wk, wv):
    m, d = x.shape
    n = wk.shape[1]
    return pl.pallas_call(
        _memproj_kernel,
        grid=(1,),
        in_specs=[_resident(x.shape), _resident(wk.shape), _resident(wv.shape)],
        out_specs=[pl.BlockSpec((m, n), lambda i: (0, 0)), pl.BlockSpec((m, n), lambda i: (0, 0))],
        out_shape=[jax.ShapeDtypeStruct((m, n), F32), jax.ShapeDtypeStruct((m, n), F32)],
        compiler_params=pltpu.CompilerParams(vmem_limit_bytes=VMEM_LIMIT),
        name="memproj",
    )(x, wk, wv)


def _conv_taps(u, prev2, prev1, cw_ref, seq_rows):
    r = u.shape[0]
    row = lax.broadcasted_iota(jnp.int32, (r, 1), 0)
    t = row if seq_rows is None else row % seq_rows
    r1 = pltpu.roll(u, 1, axis=0)
    r2 = pltpu.roll(u, 2, axis=0)
    back1 = jnp.where(t >= 1, r1, prev1)
    back2 = jnp.where(t >= 2, r2, jnp.where(t == 1, prev1, prev2))
    return cw_ref[0:1, :] * back2 + cw_ref[1:2, :] * back1 + cw_ref[2:3, :] * u


def _inproj_prompt_kernel(x_ref, w_ref, wkvt_ref, cw_ref, yc_ref, q_ref, kt_ref, vt_ref, ktb_ref, vtb_ref,
                          st_ref, carry_ref, *, dc, dsb):
    i = pl.program_id(1)

    @pl.when(i == 0)
    def _():
        carry_ref[...] = jnp.zeros(carry_ref.shape, F32)

    xb = x_ref[...].astype(BF16)
    p = _dot(xb, w_ref[...])
    gb, gc, hx = p[:, 0:dc], p[:, dc:2 * dc], p[:, 2 * dc:3 * dc]
    q = p[:, 3 * dc:3 * dc + dsb]
    u = gc * hx
    prev = carry_ref[...]
    z = _conv_taps(u, prev[6:7, :], prev[7:8, :], cw_ref, None)
    yc_ref[...] = (gb * z).astype(yc_ref.dtype)
    tail = u[u.shape[0] - 8:, :]
    carry_ref[...] = tail
    st_ref[...] = tail[6:8, :]
    q_ref[...] = (q * (SB_HEAD_DIM ** -0.5)).astype(q_ref.dtype)
    kvt = _dot_nt(wkvt_ref[...], xb)
    kt, vt = kvt[0:dsb, :], kvt[dsb:, :]
    kt_ref[...] = kt
    vt_ref[...] = vt
    ktb_ref[...] = kt.astype(BF16)
    vtb_ref[...] = vt.astype(BF16)


def _inproj_prompt(x, w_main, w_kv_t, conv_w, *, tm):
    bsz, s, d = x.shape
    dc = conv_w.shape[1]
    dsb = w_kv_t.shape[0] // 2
    kern = functools.partial(_inproj_prompt_kernel, dc=dc, dsb=dsb)
    row = lambda c: pl.BlockSpec((None, tm, c), lambda b, i: (b, i, 0))
    colm = pl.BlockSpec((None, dsb, tm), lambda b, i: (b, 0, i))
    return pl.pallas_call(
        kern,
        grid=(bsz, s // tm),
        in_specs=[row(d), _resident(w_main.shape), _resident(w_kv_t.shape), _resident(conv_w.shape)],
        out_specs=[row(dc), row(dsb), colm, colm, colm, colm,
                   pl.BlockSpec((None, CONV_WIDTH - 1, dc), lambda b, i: (b, 0, 0))],
        out_shape=[jax.ShapeDtypeStruct((bsz, s, dc), BF16),
                   jax.ShapeDtypeStruct((bsz, s, dsb), BF16),
                   jax.ShapeDtypeStruct((bsz, dsb, s), F32),
                   jax.ShapeDtypeStruct((bsz, dsb, s), F32),
                   jax.ShapeDtypeStruct((bsz, dsb, s), BF16),
                   jax.ShapeDtypeStruct((bsz, dsb, s), BF16),
                   jax.ShapeDtypeStruct((bsz, CONV_WIDTH - 1, dc), F32)],
        scratch_shapes=[pltpu.VMEM((8, dc), F32)],
        compiler_params=pltpu.CompilerParams(dimension_semantics=("arbitrary", "arbitrary"),
                                             vmem_limit_bytes=VMEM_LIMIT),
        name="inproj_prompt",
    )(x, w_main, w_kv_t, conv_w)


def _inproj_sample_kernel(x_ref, w_ref, cw_ref, c0_ref, c1_ref, yc_ref, q_ref, k_ref, v_ref, u_ref,
                          *, dc, dsb, seq_rows):
    p = _dot(x_ref[...].astype(BF16), w_ref[...])
    gb, gc, hx = p[:, 0:dc], p[:, dc:2 * dc], p[:, 2 * dc:3 * dc]
    o = 3 * dc
    u = gc * hx
    z = _conv_taps(u, c0_ref[...], c1_ref[...], cw_ref, seq_rows)
    yc_ref[...] = (gb * z).astype(yc_ref.dtype)
    u_ref[...] = u
    q_ref[...] = p[:, o:o + dsb] * (SB_HEAD_DIM ** -0.5)
    k_ref[...] = p[:, o + dsb:o + 2 * dsb]
    v_ref[...] = p[:, o + 2 * dsb:o + 3 * dsb]


def _inproj_sample(x, w_in, conv_w, c0, c1, *, seq_rows, tm):
    m, d = x.shape
    dc = conv_w.shape[1]
    dsb = (w_in.shape[1] - 3 * dc) // 3
    kern = functools.partial(_inproj_sample_kernel, dc=dc, dsb=dsb, seq_rows=seq_rows)
    row = lambda c: pl.BlockSpec((tm, c), lambda i: (i, 0))
    return pl.pallas_call(
        kern,
        grid=(m // tm,),
        in_specs=[row(d), _resident(w_in.shape), _resident(conv_w.shape), row(dc), row(dc)],
        out_specs=[row(dc), row(dsb), row(dsb), row(dsb), row(dc)],
        out_shape=[jax.ShapeDtypeStruct((m, dc), BF16),
                   jax.ShapeDtypeStruct((m, dsb), F32),
                   jax.ShapeDtypeStruct((m, dsb), F32),
                   jax.ShapeDtypeStruct((m, dsb), F32),
                   jax.ShapeDtypeStruct((m, dc), F32)],
        compiler_params=pltpu.CompilerParams(dimension_semantics=("arbitrary",),
                                             vmem_limit_bytes=VMEM_LIMIT),
        name="inproj_sample",
    )(x, w_in, conv_w, c0, c1)


def _neg_strict_upper(n, copies):
    r = lax.broadcasted_iota(jnp.int32, (copies * n, n), 0)
    c = lax.broadcasted_iota(jnp.int32, (copies * n, n), 1)
    for _ in range(copies - 1):
        r = jnp.where(r >= n, r - n, r)
    return jnp.where(r > c, -1.0, 0.0).astype(BF16)


def _neg_strict_upper2(n):
    return _neg_strict_upper(n, 2)


def _sb_scores(qb, kb, bias, transposed):
    return (_dot(qb, kb) if transposed else _dot_nt(qb, kb)) + bias


def _sb_softplus(z, valid, split):
    sp = jnp.maximum(z, 0.0) + jnp.log(1.0 + jnp.exp2(jnp.abs(z) * NEG_LOG2_E))
    spm = sp if valid is None else jnp.where(valid, sp, 0.0)
    hi = spm.astype(BF16)
    if not split:
        return z - sp, spm, hi
    lo = (spm - hi.astype(F32)).astype(BF16)
    return z - sp, spm, jnp.concatenate([hi, lo], axis=1)


def _sb_weights(log_beta, spm, neg_later, carry, valid):
    w = jnp.exp(log_beta + neg_later + jnp.tile(carry, (1, log_beta.shape[1] // carry.shape[1])))
    if valid is not None:
        w = jnp.where(valid, w, 0.0)
    return w.astype(BF16), carry - jnp.sum(spm, axis=-1, keepdims=True)


def _sb_pv(wb, vb, acc, transposed):
    return acc + (_dot_nt(wb, vb) if transposed else _dot(wb, vb))


def _sb_step(qb, kb, vb, bias, neg_upper2, carry, acc, valid, transposed):
    log_beta, spm, hilo = _sb_softplus(_sb_scores(qb, kb, bias, transposed), valid, True)
    wb, carry = _sb_weights(log_beta, spm, _dot(hilo, neg_upper2), carry, valid)
    return carry, _sb_pv(wb, vb, acc, transposed)


def _sb_prompt_part(hg, qi, bias_ref, q_ref, k_ref, v_ref, o_ref, z_scr, w_scr, carry_scr, acc_scr, *, tq,
                    n_heads):
    tk = SB_KEY_BLOCK
    neg_upper = _neg_strict_upper(tk, 1)
    qall = q_ref[...]
    heads = tuple(range(n_heads))
    row = lax.broadcasted_iota(jnp.int32, (tq, tk), 0)
    col = lax.broadcasted_iota(jnp.int32, (tq, tk), 1)

    part_col = lax.broadcasted_iota(jnp.int32, (tq, SB_HEAD_DIM), 1)
    ones_rows = jnp.where(lax.broadcasted_iota(jnp.int32, (SB_HEAD_DIM, tk), 0) < 3, 1.0, 0.0).astype(BF16)
    qs = []
    for e in heads:
        rest = jnp.full((tq, SB_HEAD_DIM), bias_ref[n_heads * hg + e], F32)
        cols = jnp.zeros((tq, SB_HEAD_DIM), F32)
        for p in range(3):
            part = rest.astype(BF16).astype(F32)
            cols = jnp.where(part_col == p, part, cols)
            rest = rest - part
        qs.append(jnp.concatenate([qall[:, e * SB_HEAD_DIM:(e + 1) * SB_HEAD_DIM], cols.astype(BF16)], axis=1))

    def head_rows(ref, e, j):
        return ref[e * SB_HEAD_DIM:(e + 1) * SB_HEAD_DIM, pl.ds(pl.multiple_of(j * tk, tk), tk)]

    def scores(e, j):
        return _dot(qs[e], jnp.concatenate([head_rows(k_ref, e, j), ones_rows], axis=0))

    def middle(valid, j_next):
        parts = []
        for e in heads:
            log_beta, spm, sp_bf = _sb_softplus(z_scr[e], valid, False)
            parts.append((log_beta, spm, _dot(sp_bf, neg_upper)))
            z_scr[e] = scores(e, j_next)
        for e, (log_beta, spm, neg_later) in zip(heads, parts):
            w_scr[e], carry_scr[e] = _sb_weights(log_beta, spm, neg_later, carry_scr[e], valid)

    def add_values(j):
        for e in heads:
            acc_scr[e] = _sb_pv(w_scr[e], head_rows(v_ref, e, j), acc_scr[e], True)

    for e in heads:
        z_scr[e] = scores(e, qi)
    carry_scr[...] = jnp.zeros(carry_scr.shape, F32)
    acc_scr[...] = jnp.zeros(acc_scr.shape, F32)
    middle(col < row, jnp.maximum(qi - 1, 0))

    def body(jj, _):
        j = qi - 1 - jj
        add_values(j + 1)
        middle(None, jnp.maximum(j - 1, 0))
        return 0

    lax.fori_loop(0, qi, body, 0)
    add_values(0)
    o_ref[...] = jnp.concatenate([acc_scr[e] for e in heads], axis=-1).astype(o_ref.dtype)


def _sb_sample_part(step, n_steps, pt_ref, bias_ref, q_ref, kn_ref, vn_ref, ck_ref, cv_ref, o_ref, kbuf, vbuf,
                    sem, *, t_new, batches_per_step, n_chunks, pages_per_chunk):
    tk = SB_KEY_BLOCK
    n_slots = kbuf.shape[0]
    lookahead = n_slots - 1
    total_chunks = n_steps * batches_per_step * n_chunks
    dsb = q_ref.shape[-1]
    nh = dsb // SB_HEAD_DIM
    rows = nh * t_new

    def page_copies(g, sl):
        bb, cc = g // n_chunks, g % n_chunks
        first = (n_chunks - 1 - cc) * pages_per_chunk
        cps = []
        for p in range(pages_per_chunk):
            page = pt_ref[bb, first + p]
            cps.append(pltpu.make_async_copy(ck_ref.at[page], kbuf.at[sl, p], sem.at[sl, 0]))
            cps.append(pltpu.make_async_copy(cv_ref.at[page], vbuf.at[sl, p], sem.at[sl, 1]))
        return cps

    @pl.when(step == 0)
    def _():
        for g in range(min(lookahead, total_chunks)):
            for cp in page_copies(g, g % n_slots):
                cp.start()

    rrow = lax.broadcasted_iota(jnp.int32, (rows, dsb), 0)
    rcol = lax.broadcasted_iota(jnp.int32, (rows, dsb), 1)
    head_mask = (rrow // t_new) == (rcol // SB_HEAD_DIM)
    bias = bias_ref[:, 0:1]
    neg_upper2 = _neg_strict_upper2(tk)
    pages_per_block = tk // PAGE_SIZE
    n_blocks = pages_per_chunk // pages_per_block
    groups = [range(g, min(g + SAMPLE_GROUP, n_blocks)) for g in range(0, n_blocks, SAMPLE_GROUP)]

    for bb in range(batches_per_step):
        batch = step * batches_per_step + bb
        own_rows = slice(bb * t_new, (bb + 1) * t_new)
        q_rep = jnp.concatenate([q_ref[own_rows, :]] * nh, axis=0)
        qbd = jnp.where(head_mask, q_rep, 0.0).astype(BF16)

        pad = jnp.zeros((128 - t_new, dsb), BF16)
        kb = jnp.concatenate([kn_ref[own_rows, :].astype(BF16), pad], axis=0)
        vb = jnp.concatenate([vn_ref[own_rows, :].astype(BF16), pad], axis=0)
        t_of_row = lax.broadcasted_iota(jnp.int32, (rows, 128), 0) % t_new
        key = lax.broadcasted_iota(jnp.int32, (rows, 128), 1)
        state = _sb_step(qbd, kb, vb, bias, _neg_strict_upper2(128), jnp.zeros((rows, 128), F32),
                         jnp.zeros((rows, dsb), F32), key < t_of_row, False)

        def chunk(c, state, batch=batch, qbd=qbd):
            carry, acc = state
            g = batch * n_chunks + c
            slot = g % n_slots

            @pl.when(g + lookahead < total_chunks)
            def _():
                for cp in page_copies(g + lookahead, (g + lookahead) % n_slots):
                    cp.start()

            for cp in page_copies(g, slot):
                cp.wait()

            def block_of(buf, i):
                first = (n_blocks - 1 - i) * pages_per_block
                pages = [buf[slot, first + p] for p in range(pages_per_block)]
                return jnp.concatenate(pages, axis=1).astype(BF16)

            zs = [_sb_scores(qbd, block_of(kbuf, i), bias, True) for i in groups[0]]
            for gi, group in enumerate(groups):
                parts = [_sb_softplus(z, None, True) for z in zs]
                neg_later = _dot(jnp.concatenate([hilo for _, _, hilo in parts], axis=0), neg_upper2)
                if gi + 1 < len(groups):
                    zs = [_sb_scores(qbd, block_of(kbuf, i), bias, True) for i in groups[gi + 1]]
                for n, (i, (log_beta, spm, _)) in enumerate(zip(group, parts)):
                    wb, carry = _sb_weights(log_beta, spm, neg_later[n * rows:(n + 1) * rows], carry, None)
                    acc = _sb_pv(wb, block_of(vbuf, i), acc, True)
            return carry, acc

        _, acc = lax.fori_loop(0, n_chunks, chunk, state)
        own = jnp.where(head_mask, acc, 0.0).reshape(nh, t_new, dsb)
        o_ref[own_rows, :] = jnp.sum(own, axis=0)


def _sb_fused_kernel(pt_ref, bias_ref, q_ref, k_ref, v_ref, brow_ref, qs_ref, kn_ref, vn_ref, ck_ref, cv_ref,
                     o_ref, os_ref, z_scr, w_scr, carry_scr, acc_scr, kbuf, vbuf, sem, *, tq, n_heads, t_new,
                     batches_per_step, n_chunks, pages_per_chunk):
    hg, qi = pl.program_id(1), pl.program_id(2)
    step = (pl.program_id(0) * pl.num_programs(1) + hg) * pl.num_programs(2) + qi
    n_steps = pl.num_programs(0) * pl.num_programs(1) * pl.num_programs(2)
    _sb_sample_part(step, n_steps, pt_ref, brow_ref, qs_ref, kn_ref, vn_ref, ck_ref, cv_ref, os_ref, kbuf, vbuf,
                    sem, t_new=t_new, batches_per_step=batches_per_step, n_chunks=n_chunks,
                    pages_per_chunk=pages_per_chunk)
    _sb_prompt_part(hg, qi, bias_ref, q_ref, k_ref, v_ref, o_ref, z_scr, w_scr, carry_scr, acc_scr, tq=tq,
                    n_heads=n_heads)


def _sb_fused(q, k_t, v_t, bias, qs, k_new, v_new, cache_k, cache_v, page_table, bias_rows, *, tq,
              heads_per_step, t_new, n_chunks, n_slots):
    bsz, s, dsb = q.shape
    nh = dsb // SB_HEAD_DIM
    assert tq == SB_KEY_BLOCK and s % tq == 0 and nh % heads_per_step == 0
    gw = heads_per_step * SB_HEAD_DIM
    grid = (bsz, nh // heads_per_step, s // tq)
    n_steps = grid[0] * grid[1] * grid[2]
    m = qs.shape[0]
    nbatch = m // t_new
    assert nbatch % n_steps == 0 and page_table.shape[1] % n_chunks == 0
    bps = nbatch // n_steps
    ppc = page_table.shape[1] // n_chunks
    kern = functools.partial(_sb_fused_kernel, tq=tq, n_heads=heads_per_step, t_new=t_new,
                             batches_per_step=bps, n_chunks=n_chunks, pages_per_chunk=ppc)

    def lin(b, hg, qi):
        return (b * grid[1] + hg) * grid[2] + qi

    kv_spec = pl.BlockSpec((None, gw, s), lambda b, hg, qi, pt: (b, hg, 0), pipeline_mode=pl.Buffered(1))
    tile = pl.BlockSpec((None, tq, gw), lambda b, hg, qi, pt: (b, qi, hg))
    srow = pl.BlockSpec((bps * t_new, dsb), lambda b, hg, qi, pt: (lin(b, hg, qi), 0))
    grid_spec = pltpu.PrefetchScalarGridSpec(
        num_scalar_prefetch=1,
        grid=grid,
        in_specs=[pl.BlockSpec(memory_space=pltpu.SMEM), tile, kv_spec, kv_spec,
                  pl.BlockSpec(bias_rows.shape, lambda b, hg, qi, pt: (0, 0)), srow, srow, srow,
                  pl.BlockSpec(memory_space=pl.ANY), pl.BlockSpec(memory_space=pl.ANY)],
        out_specs=[tile, srow],
        scratch_shapes=[pltpu.VMEM((heads_per_step, tq, SB_KEY_BLOCK), F32),
                        pltpu.VMEM((heads_per_step, tq, SB_KEY_BLOCK), BF16),
                        pltpu.VMEM((heads_per_step, tq, 128), F32),
                        pltpu.VMEM((heads_per_step, tq, SB_HEAD_DIM), F32),
                        pltpu.VMEM((n_slots, ppc, dsb, PAGE_SIZE), F32),
                        pltpu.VMEM((n_slots, ppc, dsb, PAGE_SIZE), F32),
                        pltpu.SemaphoreType.DMA((n_slots, 2))])
    return pl.pallas_call(
        kern,
        grid_spec=grid_spec,
        out_shape=[jax.ShapeDtypeStruct((bsz, s, dsb), BF16), jax.ShapeDtypeStruct((m, dsb), F32)],
        compiler_params=pltpu.CompilerParams(
            dimension_semantics=("arbitrary", "arbitrary", "arbitrary"),
            vmem_limit_bytes=VMEM_LIMIT),
        name="sb_fused",
    )(page_table, bias, q, k_t, v_t, bias_rows, qs, k_new, v_new, cache_k, cache_v)


def _mix_xattn_kernel(x_ref, yc_ref, ysb_ref, wo_ref, wq_ref, wxo_ref, mk_ref, mv_ref, g_ref, b_ref,
                      o_ref, *, alpha, n_mem_batches):
    x = x_ref[...]
    dc = yc_ref.shape[-1]
    mix = (_dot(yc_ref[...].astype(BF16), wo_ref[0:dc, :])
           + _dot(ysb_ref[...].astype(BF16), wo_ref[dc:, :]))
    x2 = _layer_norm_rows(alpha * x + mix, g_ref[1:2, :], b_ref[1:2, :])
    q = _dot(x2.astype(BF16), wq_ref[...]).astype(BF16)
    tm, dx = q.shape
    rpb = tm // n_mem_batches
    n_heads = dx // X_HEAD_DIM
    lane_tiles = X_HEAD_DIM // 128

    def mem_head(ref, j, h):
        if ref.shape[-1] == dx:
            return ref[j, :, h * X_HEAD_DIM:(h + 1) * X_HEAD_DIM].astype(BF16)
        per_tok = lane_tiles * n_heads
        parts = [ref[j, pl.ds(t * n_heads + h, ref.shape[1] // per_tok, stride=per_tok), :]
                 for t in range(lane_tiles)]
        return jnp.concatenate(parts, axis=1).astype(BF16)

    pairs = [(j, h) for j in range(n_mem_batches) for h in range(n_heads)]
    scores = [_dot_nt(q[j * rpb:(j + 1) * rpb, h * X_HEAD_DIM:(h + 1) * X_HEAD_DIM], mem_head(mk_ref, j, h))
              * (X_HEAD_DIM ** -0.5) for j, h in pairs]
    probs = []
    for s in scores:
        e = jnp.exp(s - jnp.max(s, axis=-1, keepdims=True))
        probs.append((e / jnp.sum(e, axis=-1, keepdims=True)).astype(BF16))
    outs = [_dot(a, mem_head(mv_ref, j, h)) for a, (j, h) in zip(probs, pairs)]
    rows = [jnp.concatenate(outs[j * n_heads:(j + 1) * n_heads], axis=-1) for j in range(n_mem_batches)]
    o = rows[0] if n_mem_batches == 1 else jnp.concatenate(rows, axis=0)
    xo = _dot(o.astype(BF16), wxo_ref[...])
    o_ref[...] = _layer_norm_rows(alpha * x2 + xo, g_ref[2:3, :], b_ref[2:3, :])


def _mix_xattn(x, yc, ysb, w_out, w_xq, w_xo, mk, mv, g, b, *, alpha, tm, rows_per_mem):
    m, d = x.shape
    nbm = max(tm // rows_per_mem, 1)
    kern = functools.partial(_mix_xattn_kernel, alpha=alpha, n_mem_batches=nbm)
    row = lambda c: pl.BlockSpec((tm, c), lambda i: (i, 0))
    if rows_per_mem >= tm:
        mem_spec = pl.BlockSpec((1,) + mk.shape[1:], lambda i: (i * tm // rows_per_mem, 0, 0))
    else:
        mem_spec = pl.BlockSpec((nbm,) + mk.shape[1:], lambda i: (i, 0, 0))
    return pl.pallas_call(
        kern,
        grid=(m // tm,),
        in_specs=[row(d), row(yc.shape[1]), row(ysb.shape[1]),
                  _resident(w_out.shape), _resident(w_xq.shape), _resident(w_xo.shape),
                  mem_spec, mem_spec, _resident(g.shape), _resident(b.shape)],
        out_specs=row(d),
        out_shape=jax.ShapeDtypeStruct((m, d), F32),
        compiler_params=pltpu.CompilerParams(dimension_semantics=("arbitrary",),
                                             vmem_limit_bytes=VMEM_LIMIT),
        name="mix_xattn",
    )(x, yc, ysb, w_out, w_xq, w_xo, mk, mv, g, b)


def _tile(m, want):
    return want if m % want == 0 else m


def kernel(x_prompt, x_sample, mem_prompt, cache_k, cache_v, state_conv, cache_mem_k, cache_mem_v, page_table, ln_g, ln_b, w_ffn1_gu, w_ffn1_down, w_in, conv_w, sb_bias, w_out, w_xq, w_xk, w_xv, w_xo, w_ffn2_gu, w_ffn2_down):
    depth = ln_g.shape[0]
    alpha = (2 * depth) ** 0.25
    bp, s, d = x_prompt.shape
    db, t_new, _ = x_sample.shape
    n_mem = mem_prompt.shape[1]
    dc = conv_w.shape[-1]
    dsb = (w_in.shape[-1] - 3 * dc) // 3
    nh = dsb // SB_HEAD_DIM
    d_ff = w_ffn1_down.shape[1]
    ff_chunk = d_ff // 2 if (d_ff // 2) % 128 == 0 else d_ff
    page = cache_k.shape[2]

    yp = x_prompt.reshape(bp * s, d)
    ys = x_sample.reshape(db * t_new, d)
    outs = {n: [] for n in ("kp", "vp", "cp", "mkp", "mvp", "ks", "vs", "cs")}
    for l in range(depth):
        g, b = ln_g[l], ln_b[l]
        w1gu, w1dn = w_ffn1_gu[l].astype(BF16), w_ffn1_down[l].astype(BF16)
        w2gu, w2dn = w_ffn2_gu[l].astype(BF16), w_ffn2_down[l].astype(BF16)
        win, wout = w_in[l].astype(BF16), w_out[l].astype(BF16)
        wxq, wxk, wxv, wxo = (w[l].astype(BF16) for w in (w_xq, w_xk, w_xv, w_xo))
        cw = conv_w[l]
        bias = sb_bias[l].astype(F32)
        ffn = functools.partial(_ffn_ln, alpha=alpha, ff_chunk=ff_chunk)

        mk, mv = _memproj(mem_prompt.reshape(bp * n_mem, d), wxk, wxv)
        x1p = ffn(yp, w1gu, w1dn, g[0:1], b[0:1], tm=_tile(bp * s, 512))
        ycp, qp, kt, vt, kt_bf, vt_bf, cst = _inproj_prompt(
            x1p.reshape(bp, s, d), win[:, :3 * dc + dsb], win[:, 3 * dc + dsb:].T, cw, tm=_tile(s, 512))
        ms = db * t_new
        x1 = ffn(ys, w1gu, w1dn, g[0:1], b[0:1], tm=_tile(ms, 512))
        c0 = jnp.repeat(state_conv[l][:, 0, :], t_new, axis=0)
        c1 = jnp.repeat(state_conv[l][:, 1, :], t_new, axis=0)
        yc, q, k, v, u = _inproj_sample(x1, win, cw, c0, c1, seq_rows=t_new, tm=_tile(ms, 512))
        bias_rows = jnp.broadcast_to(jnp.repeat(bias, t_new)[:, None], (nh * t_new, 128))
        ck_t = cache_k[l].transpose(0, 2, 3, 1).reshape(-1, dsb, page)
        cv_t = cache_v[l].transpose(0, 2, 3, 1).reshape(-1, dsb, page)
        ysbp, ysb = _sb_fused(qp, kt_bf, vt_bf, bias, q, k, v, ck_t, cv_t, page_table, bias_rows,
                              tq=_tile(s, SB_KEY_BLOCK), heads_per_step=min(nh, 4), t_new=t_new,
                              n_chunks=SAMPLE_CHUNKS, n_slots=SAMPLE_SLOTS)

        x3 = _mix_xattn(x1p, ycp.reshape(bp * s, dc), ysbp.reshape(bp * s, dsb), wout, wxq, wxo,
                        mk.reshape(bp, n_mem, -1), mv.reshape(bp, n_mem, -1), g, b,
                        alpha=alpha, tm=_tile(s, 512), rows_per_mem=s)
        yp = ffn(x3, w2gu, w2dn, g[3:4], b[3:4], tm=_tile(bp * s, 512))
        outs["kp"].append(kt.reshape(bp, nh, SB_HEAD_DIM, s).transpose(0, 3, 1, 2))
        outs["vp"].append(vt.reshape(bp, nh, SB_HEAD_DIM, s).transpose(0, 3, 1, 2))
        outs["cp"].append(cst)
        outs["mkp"].append(mk.reshape(bp, n_mem, -1, X_HEAD_DIM))
        outs["mvp"].append(mv.reshape(bp, n_mem, -1, X_HEAD_DIM))

        mem_per_step = 8

        def mem_rows(c):
            nxh = c.shape[2]
            c = c.reshape(db, n_mem, nxh, X_HEAD_DIM // 128, 128).transpose(0, 1, 3, 2, 4)
            return c.reshape(db, n_mem * nxh * (X_HEAD_DIM // 128), 128)

        x3 = _mix_xattn(x1, yc, ysb, wout, wxq, wxo, mem_rows(cache_mem_k[l]), mem_rows(cache_mem_v[l]),
                        g, b, alpha=alpha, tm=_tile(ms, mem_per_step * t_new), rows_per_mem=t_new)
        ys = ffn(x3, w2gu, w2dn, g[3:4], b[3:4], tm=_tile(ms, 512))
        outs["ks"].append(k.reshape(db, t_new, nh, SB_HEAD_DIM))
        outs["vs"].append(v.reshape(db, t_new, nh, SB_HEAD_DIM))
        outs["cs"].append(u.reshape(db, t_new, dc)[:, t_new - (CONV_WIDTH - 1):, :])

    st = lambda n: jnp.stack(outs[n])
    return (yp.reshape(bp, s, d), ys.reshape(db, t_new, d), st("kp"), st("vp"), st("cp"), st("mkp"),
            st("mvp"), st("ks"), st("vs"), st("cs"))
```

```python
import functools

import jax
import jax.numpy as jnp
from jax import lax
from jax.experimental import pallas as pl
from jax.experimental.pallas import tpu as pltpu

F32 = jnp.float32
BF16 = jnp.bfloat16

LN_EPS = 1e-5
NEG_LOG2_E = -1.4426950408889634
SB_HEAD_DIM = 64
X_HEAD_DIM = 256
CONV_WIDTH = 3
PAGE_SIZE = 128
MXU_WIDTH = 256
SB_KEY_BLOCK = MXU_WIDTH
SAMPLE_GROUP = 4
SAMPLE_CHUNKS = 4
SAMPLE_SLOTS = 4
VMEM_LIMIT = 56 * 1024 * 1024


def _dot(a, b):
    return jnp.dot(a, b, preferred_element_type=F32)


def _dot_nt(a, b):
    return lax.dot_general(a, b, (((1,), (1,)), ((), ())), preferred_element_type=F32)


def _layer_norm_rows(y, g, b):
    mu = jnp.mean(y, axis=-1, keepdims=True)
    d = y - mu
    var = jnp.mean(d * d, axis=-1, keepdims=True)
    return d * lax.rsqrt(var + LN_EPS) * g + b


def _resident(shape):
    nd = len(shape)
    return pl.BlockSpec(shape, lambda *_: (0,) * nd, pipeline_mode=pl.Buffered(1))


def _ffn_chunks(d_ff):
    if d_ff % MXU_WIDTH:
        return ((0, d_ff),)
    first = (d_ff // MXU_WIDTH + 1) // 2 * MXU_WIDTH
    return ((0, first), (first, d_ff)) if first < d_ff else ((0, d_ff),)


def _ffn_ln_kernel(x_ref, wgu_ref, wdn_ref, g_ref, b_ref, o_ref, *, d_ff, alpha):
    x = x_ref[...]
    xb = x.astype(BF16)
    acc = jnp.zeros(x.shape, F32)
    for lo, hi in _ffn_chunks(d_ff):
        gate = _dot(xb, wgu_ref[:, lo:hi])
        up = _dot(xb, wgu_ref[:, d_ff + lo:d_ff + hi])
        h = (gate / (1.0 + jnp.exp(-gate))) * up
        acc = acc + _dot(h.astype(BF16), wdn_ref[lo:hi, :])
    y = alpha * x + 0.5 * acc
    o_ref[...] = _layer_norm_rows(y, g_ref[...], b_ref[...])


def _ffn_ln(x, wgu, wdn, g, b, *, alpha, tm):
    m, d = x.shape
    d_ff = wdn.shape[0]
    kern = functools.partial(_ffn_ln_kernel, d_ff=d_ff, alpha=alpha)
    return pl.pallas_call(
        kern,
        grid=(m // tm,),
        in_specs=[pl.BlockSpec((tm, d), lambda i: (i, 0)),
                  _resident(wgu.shape), _resident(wdn.shape),
                  _resident(g.shape), _resident(b.shape)],
        out_specs=pl.BlockSpec((tm, d), lambda i: (i, 0)),
        out_shape=jax.ShapeDtypeStruct((m, d), F32),
        compiler_params=pltpu.CompilerParams(dimension_semantics=("arbitrary",),
                                             vmem_limit_bytes=VMEM_LIMIT),
        name="ffn_ln",
    )(x, wgu, wdn, g, b)


def _memproj_kernel(x_ref, wk_ref, wv_ref, k_ref, v_ref):
    xb = x_ref[...].astype(BF16)
    k_ref[...] = _dot(xb, wk_ref[...])
    v_ref[...] = _dot(xb, wv_ref[...])


def _memproj(x, wk, wv):
    m, d = x.shape
    n = wk.shape[1]
    return pl.pallas_call(
        _memproj_kernel,
        grid=(1,),
        in_specs=[_resident(x.shape), _resident(wk.shape), _resident(wv.shape)],
        out_specs=[pl.BlockSpec((m, n), lambda i: (0, 0)), pl.BlockSpec((m, n), lambda i: (0, 0))],
        out_shape=[jax.ShapeDtypeStruct((m, n), F32), jax.ShapeDtypeStruct((m, n), F32)],
        compiler_params=pltpu.CompilerParams(vmem_limit_bytes=VMEM_LIMIT),
        name="memproj",
    )(x, wk, wv)


def _conv_taps(u, prev2, prev1, cw_ref, seq_rows):
    r = u.shape[0]
    row = lax.broadcasted_iota(jnp.int32, (r, 1), 0)
    t = row if seq_rows is None else row % seq_rows
    r1 = pltpu.roll(u, 1, axis=0)
    r2 = pltpu.roll(u, 2, axis=0)
    back1 = jnp.where(t >= 1, r1, prev1)
    back2 = jnp.where(t >= 2, r2, jnp.where(t == 1, prev1, prev2))
    return cw_ref[0:1, :] * back2 + cw_ref[1:2, :] * back1 + cw_ref[2:3, :] * u


def _inproj_prompt_kernel(x_ref, w_ref, wkvt_ref, cw_ref, yc_ref, q_ref, kt_ref, vt_ref, ktb_ref, vtb_ref,
                          st_ref, carry_ref, *, dc, dsb):
    i = pl.program_id(1)

    @pl.when(i == 0)
    def _():
        carry_ref[...] = jnp.zeros(carry_ref.shape, F32)

    xb = x_ref[...].astype(BF16)
    p = _dot(xb, w_ref[...])
    gb, gc, hx = p[:, 0:dc], p[:, dc:2 * dc], p[:, 2 * dc:3 * dc]
    q = p[:, 3 * dc:3 * dc + dsb]
    u = gc * hx
    prev = carry_ref[...]
    z = _conv_taps(u, prev[6:7, :], prev[7:8, :], cw_ref, None)
    yc_ref[...] = (gb * z).astype(yc_ref.dtype)
    tail = u[u.shape[0] - 8:, :]
    carry_ref[...] = tail
    st_ref[...] = tail[6:8, :]
    q_ref[...] = (q * (SB_HEAD_DIM ** -0.5)).astype(q_ref.dtype)
    kvt = _dot_nt(wkvt_ref[...], xb)
    kt, vt = kvt[0:dsb, :], kvt[dsb:, :]
    kt_ref[...] = kt
    vt_ref[...] = vt
    ktb_ref[...] = kt.astype(BF16)
    vtb_ref[...] = vt.astype(BF16)


def _inproj_prompt(x, w_main, w_kv_t, conv_w, *, tm):
    bsz, s, d = x.shape
    dc = conv_w.shape[1]
    dsb = w_kv_t.shape[0] // 2
    kern = functools.partial(_inproj_prompt_kernel, dc=dc, dsb=dsb)
    row = lambda c: pl.BlockSpec((None, tm, c), lambda b, i: (b, i, 0))
    colm = pl.BlockSpec((None, dsb, tm), lambda b, i: (b, 0, i))
    return pl.pallas_call(
        kern,
        grid=(bsz, s // tm),
        in_specs=[row(d), _resident(w_main.shape), _resident(w_kv_t.shape), _resident(conv_w.shape)],
        out_specs=[row(dc), row(dsb), colm, colm, colm, colm,
                   pl.BlockSpec((None, CONV_WIDTH - 1, dc), lambda b, i: (b, 0, 0))],
        out_shape=[jax.ShapeDtypeStruct((bsz, s, dc), BF16),
                   jax.ShapeDtypeStruct((bsz, s, dsb), BF16),
                   jax.ShapeDtypeStruct((bsz, dsb, s), F32),
                   jax.ShapeDtypeStruct((bsz, dsb, s), F32),
                   jax.ShapeDtypeStruct((bsz, dsb, s), BF16),
                   jax.ShapeDtypeStruct((bsz, dsb, s), BF16),
                   jax.ShapeDtypeStruct((bsz, CONV_WIDTH - 1, dc), F32)],
        scratch_shapes=[pltpu.VMEM((8, dc), F32)],
        compiler_params=pltpu.CompilerParams(dimension_semantics=("arbitrary", "arbitrary"),
                                             vmem_limit_bytes=VMEM_LIMIT),
        name="inproj_prompt",
    )(x, w_main, w_kv_t, conv_w)


def _inproj_sample_kernel(x_ref, w_ref, cw_ref, c0_ref, c1_ref, yc_ref, q_ref, k_ref, v_ref, u_ref,
                          *, dc, dsb, seq_rows):
    p = _dot(x_ref[...].astype(BF16), w_ref[...])
    gb, gc, hx = p[:, 0:dc], p[:, dc:2 * dc], p[:, 2 * dc:3 * dc]
    o = 3 * dc
    u = gc * hx
    z = _conv_taps(u, c0_ref[...], c1_ref[...], cw_ref, seq_rows)
    yc_ref[...] = (gb * z).astype(yc_ref.dtype)
    u_ref[...] = u
    q_ref[...] = p[:, o:o + dsb] * (SB_HEAD_DIM ** -0.5)
    k_ref[...] = p[:, o + dsb:o + 2 * dsb]
    v_ref[...] = p[:, o + 2 * dsb:o + 3 * dsb]


def _inproj_sample(x, w_in, conv_w, c0, c1, *, seq_rows, tm):
    m, d = x.shape
    dc = conv_w.shape[1]
    dsb = (w_in.shape[1] - 3 * dc) // 3
    kern = functools.partial(_inproj_sample_kernel, dc=dc, dsb=dsb, seq_rows=seq_rows)
    row = lambda c: pl.BlockSpec((tm, c), lambda i: (i, 0))
    return pl.pallas_call(
        kern,
        grid=(m // tm,),
        in_specs=[row(d), _resident(w_in.shape), _resident(conv_w.shape), row(dc), row(dc)],
        out_specs=[row(dc), row(dsb), row(dsb), row(dsb), row(dc)],
        out_shape=[jax.ShapeDtypeStruct((m, dc), BF16),
                   jax.ShapeDtypeStruct((m, dsb), F32),
                   jax.ShapeDtypeStruct((m, dsb), F32),
                   jax.ShapeDtypeStruct((m, dsb), F32),
                   jax.ShapeDtypeStruct((m, dc), F32)],
        compiler_params=pltpu.CompilerParams(dimension_semantics=("arbitrary",),
                                             vmem_limit_bytes=VMEM_LIMIT),
        name="inproj_sample",
    )(x, w_in, conv_w, c0, c1)


def _neg_strict_upper(n, copies):
    r = lax.broadcasted_iota(jnp.int32, (copies * n, n), 0)
    c = lax.broadcasted_iota(jnp.int32, (copies * n, n), 1)
    for _ in range(copies - 1):
        r = jnp.where(r >= n, r - n, r)
    return jnp.where(r > c, -1.0, 0.0).astype(BF16)


def _neg_strict_upper2(n):
    return _neg_strict_upper(n, 2)


def _sb_scores(qb, kb, bias, transposed):
    return (_dot(qb, kb) if transposed else _dot_nt(qb, kb)) + bias


def _sb_softplus(z, valid, split):
    sp = jnp.maximum(z, 0.0) + jnp.log(1.0 + jnp.exp2(jnp.abs(z) * NEG_LOG2_E))
    spm = sp if valid is None else jnp.where(valid, sp, 0.0)
    hi = spm.astype(BF16)
    if not split:
        return z - sp, spm, hi
    lo = (spm - hi.astype(F32)).astype(BF16)
    return z - sp, spm, jnp.concatenate([hi, lo], axis=1)


def _sb_weights(log_beta, spm, neg_later, carry, valid):
    w = jnp.exp(log_beta + neg_later + jnp.tile(carry, (1, log_beta.shape[1] // carry.shape[1])))
    if valid is not None:
        w = jnp.where(valid, w, 0.0)
    return w.astype(BF16), carry - jnp.sum(spm, axis=-1, keepdims=True)


def _sb_pv(wb, vb, acc, transposed):
    return acc + (_dot_nt(wb, vb) if transposed else _dot(wb, vb))


def _sb_step(qb, kb, vb, bias, neg_upper2, carry, acc, valid, transposed):
    log_beta, spm, hilo = _sb_softplus(_sb_scores(qb, kb, bias, transposed), valid, True)
    wb, carry = _sb_weights(log_beta, spm, _dot(hilo, neg_upper2), carry, valid)
    return carry, _sb_pv(wb, vb, acc, transposed)


def _sb_prompt_part(hg, qi, bias_ref, q_ref, k_ref, v_ref, o_ref, z_scr, w_scr, carry_scr, acc_scr, *, tq,
                    n_heads):
    tk = SB_KEY_BLOCK
    neg_upper = _neg_strict_upper(tk, 1)
    qall = q_ref[...]
    heads = tuple(range(n_heads))
    row = lax.broadcasted_iota(jnp.int32, (tq, tk), 0)
    col = lax.broadcasted_iota(jnp.int32, (tq, tk), 1)

    part_col = lax.broadcasted_iota(jnp.int32, (tq, SB_HEAD_DIM), 1)
    ones_rows = jnp.where(lax.broadcasted_iota(jnp.int32, (SB_HEAD_DIM, tk), 0) < 3, 1.0, 0.0).astype(BF16)
    qs = []
    for e in heads:
        rest = jnp.full((tq, SB_HEAD_DIM), bias_ref[n_heads * hg + e], F32)
        cols = jnp.zeros((tq, SB_HEAD_DIM), F32)
        for p in range(3):
            part = rest.astype(BF16).astype(F32)
            cols = jnp.where(part_col == p, part, cols)
            rest = rest - part
        qs.append(jnp.concatenate([qall[:, e * SB_HEAD_DIM:(e + 1) * SB_HEAD_DIM], cols.astype(BF16)], axis=1))

    def head_rows(ref, e, j):
        return ref[e * SB_HEAD_DIM:(e + 1) * SB_HEAD_DIM, pl.ds(pl.multiple_of(j * tk, tk), tk)]

    def scores(e, j):
        return _dot(qs[e], jnp.concatenate([head_rows(k_ref, e, j), ones_rows], axis=0))

    def middle(valid, j_next):
        parts = []
        for e in heads:
            log_beta, spm, sp_bf = _sb_softplus(z_scr[e], valid, False)
            parts.append((log_beta, spm, _dot(sp_bf, neg_upper)))
            z_scr[e] = scores(e, j_next)
        for e, (log_beta, spm, neg_later) in zip(heads, parts):
            w_scr[e], carry_scr[e] = _sb_weights(log_beta, spm, neg_later, carry_scr[e], valid)

    def add_values(j):
        for e in heads:
            acc_scr[e] = _sb_pv(w_scr[e], head_rows(v_ref, e, j), acc_scr[e], True)

    for e in heads:
        z_scr[e] = scores(e, qi)
    carry_scr[...] = jnp.zeros(carry_scr.shape, F32)
    acc_scr[...] = jnp.zeros(acc_scr.shape, F32)
    middle(col < row, jnp.maximum(qi - 1, 0))

    def step(j):
        add_values(j + 1)
        middle(None, jnp.maximum(j - 1, 0))

    odd = qi % 2

    @pl.when(odd == 1)
    def _():
        step(qi - 1)

    def body(jj, _):
        j = qi - 1 - odd - 2 * jj
        step(j)
        step(j - 1)
        return 0

    lax.fori_loop(0, qi // 2, body, 0)
    add_values(0)
    o_ref[...] = jnp.concatenate([acc_scr[e] for e in heads], axis=-1).astype(o_ref.dtype)


def _sb_sample_part(step, n_steps, pt_ref, bias_ref, q_ref, kn_ref, vn_ref, ck_ref, cv_ref, o_ref, kbuf, vbuf,
                    sem, *, t_new, batches_per_step, n_chunks, pages_per_chunk):
    tk = SB_KEY_BLOCK
    n_slots = kbuf.shape[0]
    lookahead = n_slots - 1
    total_chunks = n_steps * batches_per_step * n_chunks
    dsb = q_ref.shape[-1]
    nh = dsb // SB_HEAD_DIM
    rows = nh * t_new

    def page_copies(g, sl):
        bb, cc = g // n_chunks, g % n_chunks
        first = (n_chunks - 1 - cc) * pages_per_chunk
        cps = []
        for p in range(pages_per_chunk):
            page = pt_ref[bb, first + p]
            cps.append(pltpu.make_async_copy(ck_ref.at[page], kbuf.at[sl, p], sem.at[sl, 0]))
            cps.append(pltpu.make_async_copy(cv_ref.at[page], vbuf.at[sl, p], sem.at[sl, 1]))
        return cps

    @pl.when(step == 0)
    def _():
        for g in range(min(lookahead, total_chunks)):
            for cp in page_copies(g, g % n_slots):
                cp.start()

    rrow = lax.broadcasted_iota(jnp.int32, (rows, dsb), 0)
    rcol = lax.broadcasted_iota(jnp.int32, (rows, dsb), 1)
    head_mask = (rrow // t_new) == (rcol // SB_HEAD_DIM)
    bias = bias_ref[:, 0:1]
    neg_upper2 = _neg_strict_upper2(tk)
    pages_per_block = tk // PAGE_SIZE
    n_blocks = pages_per_chunk // pages_per_block
    groups = [range(g, min(g + SAMPLE_GROUP, n_blocks)) for g in range(0, n_blocks, SAMPLE_GROUP)]

    for bb in range(batches_per_step):
        batch = step * batches_per_step + bb
        own_rows = slice(bb * t_new, (bb + 1) * t_new)
        q_rep = jnp.concatenate([q_ref[own_rows, :]] * nh, axis=0)
        qbd = jnp.where(head_mask, q_rep, 0.0).astype(BF16)

        pad = jnp.zeros((128 - t_new, dsb), BF16)
        kb = jnp.concatenate([kn_ref[own_rows, :].astype(BF16), pad], axis=0)
        vb = jnp.concatenate([vn_ref[own_rows, :].astype(BF16), pad], axis=0)
        t_of_row = lax.broadcasted_iota(jnp.int32, (rows, 128), 0) % t_new
        key = lax.broadcasted_iota(jnp.int32, (rows, 128), 1)
        state = _sb_step(qbd, kb, vb, bias, _neg_strict_upper2(128), jnp.zeros((rows, 128), F32),
                         jnp.zeros((rows, dsb), F32), key < t_of_row, False)

        def chunk(c, state, batch=batch, qbd=qbd):
            carry, acc = state
            g = batch * n_chunks + c
            slot = g % n_slots

            @pl.when(g + lookahead < total_chunks)
            def _():
                for cp in page_copies(g + lookahead, (g + lookahead) % n_slots):
                    cp.start()

            for cp in page_copies(g, slot):
                cp.wait()

            def block_of(buf, i):
                first = (n_blocks - 1 - i) * pages_per_block
                pages = [buf[slot, first + p] for p in range(pages_per_block)]
                return jnp.concatenate(pages, axis=1).astype(BF16)

            zs = [_sb_scores(qbd, block_of(kbuf, i), bias, True) for i in groups[0]]
            for gi, group in enumerate(groups):
                parts = [_sb_softplus(z, None, True) for z in zs]
                neg_later = _dot(jnp.concatenate([hilo for _, _, hilo in parts], axis=0), neg_upper2)
                if gi + 1 < len(groups):
                    zs = [_sb_scores(qbd, block_of(kbuf, i), bias, True) for i in groups[gi + 1]]
                for n, (i, (log_beta, spm, _)) in enumerate(zip(group, parts)):
                    wb, carry = _sb_weights(log_beta, spm, neg_later[n * rows:(n + 1) * rows], carry, None)
                    acc = _sb_pv(wb, block_of(vbuf, i), acc, True)
            return carry, acc

        _, acc = lax.fori_loop(0, n_chunks, chunk, state)
        own = jnp.where(head_mask, acc, 0.0).reshape(nh, t_new, dsb)
        o_ref[own_rows, :] = jnp.sum(own, axis=0)


def _sb_fused_kernel(pt_ref, bias_ref, q_ref, k_ref, v_ref, brow_ref, qs_ref, kn_ref, vn_ref, ck_ref, cv_ref,
                     o_ref, os_ref, z_scr, w_scr, carry_scr, acc_scr, kbuf, vbuf, sem, *, tq, n_heads, t_new,
                     batches_per_step, n_chunks, pages_per_chunk):
    hg, qi = pl.program_id(1), pl.program_id(2)
    step = (pl.program_id(0) * pl.num_programs(1) + hg) * pl.num_programs(2) + qi
    n_steps = pl.num_programs(0) * pl.num_programs(1) * pl.num_programs(2)
    _sb_sample_part(step, n_steps, pt_ref, brow_ref, qs_ref, kn_ref, vn_ref, ck_ref, cv_ref, os_ref, kbuf, vbuf,
                    sem, t_new=t_new, batches_per_step=batches_per_step, n_chunks=n_chunks,
                    pages_per_chunk=pages_per_chunk)
    _sb_prompt_part(hg, qi, bias_ref, q_ref, k_ref, v_ref, o_ref, z_scr, w_scr, carry_scr, acc_scr, tq=tq,
                    n_heads=n_heads)


def _sb_fused(q, k_t, v_t, bias, qs, k_new, v_new, cache_k, cache_v, page_table, bias_rows, *, tq,
              heads_per_step, t_new, n_chunks, n_slots):
    bsz, s, dsb = q.shape
    nh = dsb // SB_HEAD_DIM
    assert tq == SB_KEY_BLOCK and s % tq == 0 and nh % heads_per_step == 0
    gw = heads_per_step * SB_HEAD_DIM
    grid = (bsz, nh // heads_per_step, s // tq)
    n_steps = grid[0] * grid[1] * grid[2]
    m = qs.shape[0]
    nbatch = m // t_new
    assert nbatch % n_steps == 0 and page_table.shape[1] % n_chunks == 0
    bps = nbatch // n_steps
    ppc = page_table.shape[1] // n_chunks
    kern = functools.partial(_sb_fused_kernel, tq=tq, n_heads=heads_per_step, t_new=t_new,
                             batches_per_step=bps, n_chunks=n_chunks, pages_per_chunk=ppc)

    def lin(b, hg, qi):
        return (b * grid[1] + hg) * grid[2] + qi

    kv_spec = pl.BlockSpec((None, gw, s), lambda b, hg, qi, pt: (b, hg, 0), pipeline_mode=pl.Buffered(1))
    tile = pl.BlockSpec((None, tq, gw), lambda b, hg, qi, pt: (b, qi, hg))
    srow = pl.BlockSpec((bps * t_new, dsb), lambda b, hg, qi, pt: (lin(b, hg, qi), 0))
    grid_spec = pltpu.PrefetchScalarGridSpec(
        num_scalar_prefetch=1,
        grid=grid,
        in_specs=[pl.BlockSpec(memory_space=pltpu.SMEM), tile, kv_spec, kv_spec,
                  pl.BlockSpec(bias_rows.shape, lambda b, hg, qi, pt: (0, 0)), srow, srow, srow,
                  pl.BlockSpec(memory_space=pl.ANY), pl.BlockSpec(memory_space=pl.ANY)],
        out_specs=[tile, srow],
        scratch_shapes=[pltpu.VMEM((heads_per_step, tq, SB_KEY_BLOCK), F32),
                        pltpu.VMEM((heads_per_step, tq, SB_KEY_BLOCK), BF16),
                        pltpu.VMEM((heads_per_step, tq, 128), F32),
                        pltpu.VMEM((heads_per_step, tq, SB_HEAD_DIM), F32),
                        pltpu.VMEM((n_slots, ppc, dsb, PAGE_SIZE), F32),
                        pltpu.VMEM((n_slots, ppc, dsb, PAGE_SIZE), F32),
                        pltpu.SemaphoreType.DMA((n_slots, 2))])
    return pl.pallas_call(
        kern,
        grid_spec=grid_spec,
        out_shape=[jax.ShapeDtypeStruct((bsz, s, dsb), BF16), jax.ShapeDtypeStruct((m, dsb), F32)],
        compiler_params=pltpu.CompilerParams(
            dimension_semantics=("arbitrary", "arbitrary", "arbitrary"),
            vmem_limit_bytes=VMEM_LIMIT),
        name="sb_fused",
    )(page_table, bias, q, k_t, v_t, bias_rows, qs, k_new, v_new, cache_k, cache_v)


def _mix_xattn_kernel(x_ref, yc_ref, ysb_ref, wo_ref, wq_ref, wxo_ref, mk_ref, mv_ref, g_ref, b_ref,
                      o_ref, *, alpha, n_mem_batches):
    x = x_ref[...]
    dc = yc_ref.shape[-1]
    mix = (_dot(yc_ref[...].astype(BF16), wo_ref[0:dc, :])
           + _dot(ysb_ref[...].astype(BF16), wo_ref[dc:, :]))
    x2 = _layer_norm_rows(alpha * x + mix, g_ref[1:2, :], b_ref[1:2, :])
    q = _dot(x2.astype(BF16), wq_ref[...]).astype(BF16)
    tm, dx = q.shape
    rpb = tm // n_mem_batches
    n_heads = dx // X_HEAD_DIM
    lane_tiles = X_HEAD_DIM // 128

    def mem_head(ref, j, h):
        if ref.shape[-1] == dx:
            return ref[j, :, h * X_HEAD_DIM:(h + 1) * X_HEAD_DIM].astype(BF16)
        per_tok = lane_tiles * n_heads
        parts = [ref[j, pl.ds(t * n_heads + h, ref.shape[1] // per_tok, stride=per_tok), :]
                 for t in range(lane_tiles)]
        return jnp.concatenate(parts, axis=1).astype(BF16)

    pairs = [(j, h) for j in range(n_mem_batches) for h in range(n_heads)]
    scores = [_dot_nt(q[j * rpb:(j + 1) * rpb, h * X_HEAD_DIM:(h + 1) * X_HEAD_DIM], mem_head(mk_ref, j, h))
              * (X_HEAD_DIM ** -0.5) for j, h in pairs]
    probs = []
    for s in scores:
        e = jnp.exp(s - jnp.max(s, axis=-1, keepdims=True))
        probs.append((e / jnp.sum(e, axis=-1, keepdims=True)).astype(BF16))
    outs = [_dot(a, mem_head(mv_ref, j, h)) for a, (j, h) in zip(probs, pairs)]
    rows = [jnp.concatenate(outs[j * n_heads:(j + 1) * n_heads], axis=-1) for j in range(n_mem_batches)]
    o = rows[0] if n_mem_batches == 1 else jnp.concatenate(rows, axis=0)
    xo = _dot(o.astype(BF16), wxo_ref[...])
    o_ref[...] = _layer_norm_rows(alpha * x2 + xo, g_ref[2:3, :], b_ref[2:3, :])


def _mix_xattn(x, yc, ysb, w_out, w_xq, w_xo, mk, mv, g, b, *, alpha, tm, rows_per_mem):
    m, d = x.shape
    nbm = max(tm // rows_per_mem, 1)
    kern = functools.partial(_mix_xattn_kernel, alpha=alpha, n_mem_batches=nbm)
    row = lambda c: pl.BlockSpec((tm, c), lambda i: (i, 0))
    if rows_per_mem >= tm:
        mem_spec = pl.BlockSpec((1,) + mk.shape[1:], lambda i: (i * tm // rows_per_mem, 0, 0))
    else:
        mem_spec = pl.BlockSpec((nbm,) + mk.shape[1:], lambda i: (i, 0, 0))
    return pl.pallas_call(
        kern,
        grid=(m // tm,),
        in_specs=[row(d), row(yc.shape[1]), row(ysb.shape[1]),
                  _resident(w_out.shape), _resident(w_xq.shape), _resident(w_xo.shape),
                  mem_spec, mem_spec, _resident(g.shape), _resident(b.shape)],
        out_specs=row(d),
        out_shape=jax.ShapeDtypeStruct((m, d), F32),
        compiler_params=pltpu.CompilerParams(dimension_semantics=("arbitrary",),
                                             vmem_limit_bytes=VMEM_LIMIT),
        name="mix_xattn",
    )(x, yc, ysb, w_out, w_xq, w_xo, mk, mv, g, b)


def _tile(m, want):
    return want if m % want == 0 else m


def kernel(x_prompt, x_sample, mem_prompt, cache_k, cache_v, state_conv, cache_mem_k, cache_mem_v, page_table, ln_g, ln_b, w_ffn1_gu, w_ffn1_down, w_in, conv_w, sb_bias, w_out, w_xq, w_xk, w_xv, w_xo, w_ffn2_gu, w_ffn2_down):
    depth = ln_g.shape[0]
    alpha = (2 * depth) ** 0.25
    bp, s, d = x_prompt.shape
    db, t_new, _ = x_sample.shape
    n_mem = mem_prompt.shape[1]
    dc = conv_w.shape[-1]
    dsb = (w_in.shape[-1] - 3 * dc) // 3
    nh = dsb // SB_HEAD_DIM
    d_ff = w_ffn1_down.shape[1]
    page = cache_k.shape[2]

    yp = x_prompt.reshape(bp * s, d)
    ys = x_sample.reshape(db * t_new, d)
    outs = {n: [] for n in ("kp", "vp", "cp", "mkp", "mvp", "ks", "vs", "cs")}
    for l in range(depth):
        g, b = ln_g[l], ln_b[l]
        w1gu, w1dn = w_ffn1_gu[l].astype(BF16), w_ffn1_down[l].astype(BF16)
        w2gu, w2dn = w_ffn2_gu[l].astype(BF16), w_ffn2_down[l].astype(BF16)
        win, wout = w_in[l].astype(BF16), w_out[l].astype(BF16)
        wxq, wxk, wxv, wxo = (w[l].astype(BF16) for w in (w_xq, w_xk, w_xv, w_xo))
        cw = conv_w[l]
        bias = sb_bias[l].astype(F32)
        ffn = functools.partial(_ffn_ln, alpha=alpha)

        mk, mv = _memproj(mem_prompt.reshape(bp * n_mem, d), wxk, wxv)
        x1p = ffn(yp, w1gu, w1dn, g[0:1], b[0:1], tm=_tile(bp * s, 512))
        ycp, qp, kt, vt, kt_bf, vt_bf, cst = _inproj_prompt(
            x1p.reshape(bp, s, d), win[:, :3 * dc + dsb], win[:, 3 * dc + dsb:].T, cw, tm=_tile(s, 512))
        ms = db * t_new
        x1 = ffn(ys, w1gu, w1dn, g[0:1], b[0:1], tm=_tile(ms, 512))
        c0 = jnp.repeat(state_conv[l][:, 0, :], t_new, axis=0)
        c1 = jnp.repeat(state_conv[l][:, 1, :], t_new, axis=0)
        yc, q, k, v, u = _inproj_sample(x1, win, cw, c0, c1, seq_rows=t_new, tm=_tile(ms, 512))
        bias_rows = jnp.broadcast_to(jnp.repeat(bias, t_new)[:, None], (nh * t_new, 128))
        ck_t = cache_k[l].transpose(0, 2, 3, 1).reshape(-1, dsb, page)
        cv_t = cache_v[l].transpose(0, 2, 3, 1).reshape(-1, dsb, page)
        ysbp, ysb = _sb_fused(qp, kt_bf, vt_bf, bias, q, k, v, ck_t, cv_t, page_table, bias_rows,
                              tq=_tile(s, SB_KEY_BLOCK), heads_per_step=min(nh, 4), t_new=t_new,
                              n_chunks=SAMPLE_CHUNKS, n_slots=SAMPLE_SLOTS)

        x3 = _mix_xattn(x1p, ycp.reshape(bp * s, dc), ysbp.reshape(bp * s, dsb), wout, wxq, wxo,
                        mk.reshape(bp, n_mem, -1), mv.reshape(bp, n_mem, -1), g, b,
                        alpha=alpha, tm=_tile(s, 512), rows_per_mem=s)
        yp = ffn(x3, w2gu, w2dn, g[3:4], b[3:4], tm=_tile(bp * s, 512))
        outs["kp"].append(kt.reshape(bp, nh, SB_HEAD_DIM, s).transpose(0, 3, 1, 2))
        outs["vp"].append(vt.reshape(bp, nh, SB_HEAD_DIM, s).transpose(0, 3, 1, 2))
        outs["cp"].append(cst)
        outs["mkp"].append(mk.reshape(bp, n_mem, -1, X_HEAD_DIM))
        outs["mvp"].append(mv.reshape(bp, n_mem, -1, X_HEAD_DIM))

        mem_per_step = 8

        def mem_rows(c):
            nxh = c.shape[2]
            c = c.reshape(db, n_mem, nxh, X_HEAD_DIM // 128, 128).transpose(0, 1, 3, 2, 4)
            return c.reshape(db, n_mem * nxh * (X_HEAD_DIM // 128), 128)

        x3 = _mix_xattn(x1, yc, ysb, wout, wxq, wxo, mem_rows(cache_mem_k[l]), mem_rows(cache_mem_v[l]),
                        g, b, alpha=alpha, tm=_tile(ms, mem_per_step * t_new), rows_per_mem=t_new)
        ys = ffn(x3, w2gu, w2dn, g[3:4], b[3:4], tm=_tile(ms, 512))
        outs["ks"].append(k.reshape(db, t_new, nh, SB_HEAD_DIM))
        outs["vs"].append(v.reshape(db, t_new, nh, SB_HEAD_DIM))
        outs["cs"].append(u.reshape(db, t_new, dc)[:, t_new - (CONV_WIDTH - 1):, :])

    st = lambda n: jnp.stack(outs[n])
    return (yp.reshape(bp, s, d), ys.reshape(db, t_new, d), st("kp"), st("vp"), st("cp"), st("mkp"),
            st("mvp"), st("ks"), st("vs"), st("cs"))
```

```python
import functools

import jax
import jax.numpy as jnp
from jax import lax
from jax.experimental import pallas as pl
from jax.experimental.pallas import tpu as pltpu

F32 = jnp.float32
BF16 = jnp.bfloat16

LN_EPS = 1e-5
NEG_LOG2_E = -1.4426950408889634
SB_HEAD_DIM = 64
X_HEAD_DIM = 256
CONV_WIDTH = 3
PAGE_SIZE = 128
MXU_WIDTH = 256
SB_KEY_BLOCK = MXU_WIDTH
SAMPLE_GROUP = 4
SAMPLE_CHUNKS = 4
SAMPLE_SLOTS = 4
ROW_TILE = 1024
SAMPLE_ROW_TILE = 512
VMEM_LIMIT = 56 * 1024 * 1024


def _dot(a, b):
    return jnp.dot(a, b, preferred_element_type=F32)


def _dot_nt(a, b):
    return lax.dot_general(a, b, (((1,), (1,)), ((), ())), preferred_element_type=F32)


def _layer_norm_rows(y, g, b):
    mu = jnp.mean(y, axis=-1, keepdims=True)
    d = y - mu
    var = jnp.mean(d * d, axis=-1, keepdims=True)
    return d * lax.rsqrt(var + LN_EPS) * g + b


def _resident(shape):
    nd = len(shape)
    return pl.BlockSpec(shape, lambda *_: (0,) * nd, pipeline_mode=pl.Buffered(1))


def _ffn_chunks(d_ff):
    if d_ff % MXU_WIDTH:
        return ((0, d_ff),)
    first = (d_ff // MXU_WIDTH + 1) // 2 * MXU_WIDTH
    return ((0, first), (first, d_ff)) if first < d_ff else ((0, d_ff),)


def _ffn_ln_kernel(x_ref, wgu_ref, wdn_ref, g_ref, b_ref, o_ref, *, d_ff, alpha):
    x = x_ref[...]
    xb = x.astype(BF16)
    acc = jnp.zeros(x.shape, F32)
    for lo, hi in _ffn_chunks(d_ff):
        gate = _dot(xb, wgu_ref[:, lo:hi])
        up = _dot(xb, wgu_ref[:, d_ff + lo:d_ff + hi])
        h = (gate / (1.0 + jnp.exp(-gate))) * up
        acc = acc + _dot(h.astype(BF16), wdn_ref[lo:hi, :])
    y = alpha * x + 0.5 * acc
    o_ref[...] = _layer_norm_rows(y, g_ref[...], b_ref[...])


def _ffn_ln(x, wgu, wdn, g, b, *, alpha, tm):
    m, d = x.shape
    d_ff = wdn.shape[0]
    kern = functools.partial(_ffn_ln_kernel, d_ff=d_ff, alpha=alpha)
    return pl.pallas_call(
        kern,
        grid=(m // tm,),
        in_specs=[pl.BlockSpec((tm, d), lambda i: (i, 0)),
                  _resident(wgu.shape), _resident(wdn.shape),
                  _resident(g.shape), _resident(b.shape)],
        out_specs=pl.BlockSpec((tm, d), lambda i: (i, 0)),
        out_shape=jax.ShapeDtypeStruct((m, d), F32),
        compiler_params=pltpu.CompilerParams(dimension_semantics=("arbitrary",),
                                             vmem_limit_bytes=VMEM_LIMIT),
        name="ffn_ln",
    )(x, wgu, wdn, g, b)


def _memproj_kernel(x_ref, wk_ref, wv_ref, k_ref, v_ref):
    xb = x_ref[...].astype(BF16)
    k_ref[...] = _dot(xb, wk_ref[...])
    v_ref[...] = _dot(xb, wv_ref[...])


def _memproj(x, wk, wv):
    m, d = x.shape
    n = wk.shape[1]
    return pl.pallas_call(
        _memproj_kernel,
        grid=(1,),
        in_specs=[_resident(x.shape), _resident(wk.shape), _resident(wv.shape)],
        out_specs=[pl.BlockSpec((m, n), lambda i: (0, 0)), pl.BlockSpec((m, n), lambda i: (0, 0))],
        out_shape=[jax.ShapeDtypeStruct((m, n), F32), jax.ShapeDtypeStruct((m, n), F32)],
        compiler_params=pltpu.CompilerParams(vmem_limit_bytes=VMEM_LIMIT),
        name="memproj",
    )(x, wk, wv)


def _conv_taps(u, prev2, prev1, cw_ref, seq_rows):
    r = u.shape[0]
    row = lax.broadcasted_iota(jnp.int32, (r, 1), 0)
    t = row if seq_rows is None else row % seq_rows
    r1 = pltpu.roll(u, 1, axis=0)
    r2 = pltpu.roll(u, 2, axis=0)
    back1 = jnp.where(t >= 1, r1, prev1)
    back2 = jnp.where(t >= 2, r2, jnp.where(t == 1, prev1, prev2))
    return cw_ref[0:1, :] * back2 + cw_ref[1:2, :] * back1 + cw_ref[2:3, :] * u


def _inproj_prompt_kernel(x_ref, w_ref, wkvt_ref, cw_ref, yc_ref, q_ref, kt_ref, vt_ref, ktb_ref, vtb_ref,
                          st_ref, carry_ref, *, dc, dsb):
    i = pl.program_id(1)

    @pl.when(i == 0)
    def _():
        carry_ref[...] = jnp.zeros(carry_ref.shape, F32)

    xb = x_ref[...].astype(BF16)
    p = _dot(xb, w_ref[...])
    gb, gc, hx = p[:, 0:dc], p[:, dc:2 * dc], p[:, 2 * dc:3 * dc]
    q = p[:, 3 * dc:3 * dc + dsb]
    u = gc * hx
    prev = carry_ref[...]
    z = _conv_taps(u, prev[6:7, :], prev[7:8, :], cw_ref, None)
    yc_ref[...] = (gb * z).astype(yc_ref.dtype)
    tail = u[u.shape[0] - 8:, :]
    carry_ref[...] = tail
    st_ref[...] = tail[6:8, :]
    q_ref[...] = (q * (SB_HEAD_DIM ** -0.5)).astype(q_ref.dtype)
    kvt = _dot_nt(wkvt_ref[...], xb)
    kt, vt = kvt[0:dsb, :], kvt[dsb:, :]
    kt_ref[...] = kt
    vt_ref[...] = vt
    ktb_ref[...] = kt.astype(BF16)
    vtb_ref[...] = vt.astype(BF16)


def _inproj_prompt(x, w_main, w_kv_t, conv_w, *, tm):
    bsz, s, d = x.shape
    dc = conv_w.shape[1]
    dsb = w_kv_t.shape[0] // 2
    kern = functools.partial(_inproj_prompt_kernel, dc=dc, dsb=dsb)
    row = lambda c: pl.BlockSpec((None, tm, c), lambda b, i: (b, i, 0))
    colm = pl.BlockSpec((None, dsb, tm), lambda b, i: (b, 0, i))
    return pl.pallas_call(
        kern,
        grid=(bsz, s // tm),
        in_specs=[row(d), _resident(w_main.shape), _resident(w_kv_t.shape), _resident(conv_w.shape)],
        out_specs=[row(dc), row(dsb), colm, colm, colm, colm,
                   pl.BlockSpec((None, CONV_WIDTH - 1, dc), lambda b, i: (b, 0, 0))],
        out_shape=[jax.ShapeDtypeStruct((bsz, s, dc), BF16),
                   jax.ShapeDtypeStruct((bsz, s, dsb), BF16),
                   jax.ShapeDtypeStruct((bsz, dsb, s), F32),
                   jax.ShapeDtypeStruct((bsz, dsb, s), F32),
                   jax.ShapeDtypeStruct((bsz, dsb, s), BF16),
                   jax.ShapeDtypeStruct((bsz, dsb, s), BF16),
                   jax.ShapeDtypeStruct((bsz, CONV_WIDTH - 1, dc), F32)],
        scratch_shapes=[pltpu.VMEM((8, dc), F32)],
        compiler_params=pltpu.CompilerParams(dimension_semantics=("arbitrary", "arbitrary"),
                                             vmem_limit_bytes=VMEM_LIMIT),
        name="inproj_prompt",
    )(x, w_main, w_kv_t, conv_w)


def _inproj_sample_kernel(x_ref, w_ref, cw_ref, c0_ref, c1_ref, yc_ref, q_ref, k_ref, v_ref, u_ref,
                          *, dc, dsb, seq_rows):
    p = _dot(x_ref[...].astype(BF16), w_ref[...])
    gb, gc, hx = p[:, 0:dc], p[:, dc:2 * dc], p[:, 2 * dc:3 * dc]
    o = 3 * dc
    u = gc * hx
    z = _conv_taps(u, c0_ref[...], c1_ref[...], cw_ref, seq_rows)
    yc_ref[...] = (gb * z).astype(yc_ref.dtype)
    u_ref[...] = u
    q_ref[...] = p[:, o:o + dsb] * (SB_HEAD_DIM ** -0.5)
    k_ref[...] = p[:, o + dsb:o + 2 * dsb]
    v_ref[...] = p[:, o + 2 * dsb:o + 3 * dsb]


def _inproj_sample(x, w_in, conv_w, c0, c1, *, seq_rows, tm):
    m, d = x.shape
    dc = conv_w.shape[1]
    dsb = (w_in.shape[1] - 3 * dc) // 3
    kern = functools.partial(_inproj_sample_kernel, dc=dc, dsb=dsb, seq_rows=seq_rows)
    row = lambda c: pl.BlockSpec((tm, c), lambda i: (i, 0))
    return pl.pallas_call(
        kern,
        grid=(m // tm,),
        in_specs=[row(d), _resident(w_in.shape), _resident(conv_w.shape), row(dc), row(dc)],
        out_specs=[row(dc), row(dsb), row(dsb), row(dsb), row(dc)],
        out_shape=[jax.ShapeDtypeStruct((m, dc), BF16),
                   jax.ShapeDtypeStruct((m, dsb), F32),
                   jax.ShapeDtypeStruct((m, dsb), F32),
                   jax.ShapeDtypeStruct((m, dsb), F32),
                   jax.ShapeDtypeStruct((m, dc), F32)],
        compiler_params=pltpu.CompilerParams(dimension_semantics=("arbitrary",),
                                             vmem_limit_bytes=VMEM_LIMIT),
        name="inproj_sample",
    )(x, w_in, conv_w, c0, c1)


def _neg_strict_upper(n, copies):
    r = lax.broadcasted_iota(jnp.int32, (copies * n, n), 0)
    c = lax.broadcasted_iota(jnp.int32, (copies * n, n), 1)
    for _ in range(copies - 1):
        r = jnp.where(r >= n, r - n, r)
    return jnp.where(r > c, -1.0, 0.0).astype(BF16)


def _neg_strict_upper2(n):
    return _neg_strict_upper(n, 2)


def _sb_scores(qb, kb, bias, transposed):
    return (_dot(qb, kb) if transposed else _dot_nt(qb, kb)) + bias


def _sb_softplus(z, valid, split):
    sp = jnp.maximum(z, 0.0) + jnp.log(1.0 + jnp.exp2(jnp.abs(z) * NEG_LOG2_E))
    spm = sp if valid is None else jnp.where(valid, sp, 0.0)
    hi = spm.astype(BF16)
    if not split:
        return z - sp, spm, hi
    lo = (spm - hi.astype(F32)).astype(BF16)
    return z - sp, spm, jnp.concatenate([hi, lo], axis=1)


def _sb_weights(log_beta, spm, neg_later, carry, valid):
    w = jnp.exp(log_beta + neg_later + jnp.tile(carry, (1, log_beta.shape[1] // carry.shape[1])))
    if valid is not None:
        w = jnp.where(valid, w, 0.0)
    return w.astype(BF16), carry - jnp.sum(spm, axis=-1, keepdims=True)


def _sb_pv(wb, vb, acc, transposed):
    return acc + (_dot_nt(wb, vb) if transposed else _dot(wb, vb))


def _sb_step(qb, kb, vb, bias, neg_upper2, carry, acc, valid, transposed):
    log_beta, spm, hilo = _sb_softplus(_sb_scores(qb, kb, bias, transposed), valid, True)
    wb, carry = _sb_weights(log_beta, spm, _dot(hilo, neg_upper2), carry, valid)
    return carry, _sb_pv(wb, vb, acc, transposed)


def _sb_prompt_part(hg, qi, bias_ref, q_ref, k_ref, v_ref, o_ref, z_scr, w_scr, carry_scr, acc_scr, *, tq,
                    n_heads):
    tk = SB_KEY_BLOCK
    neg_upper = _neg_strict_upper(tk, 1)
    qall = q_ref[...]
    heads = tuple(range(n_heads))
    row = lax.broadcasted_iota(jnp.int32, (tq, tk), 0)
    col = lax.broadcasted_iota(jnp.int32, (tq, tk), 1)

    part_col = lax.broadcasted_iota(jnp.int32, (tq, SB_HEAD_DIM), 1)
    ones_rows = jnp.where(lax.broadcasted_iota(jnp.int32, (SB_HEAD_DIM, tk), 0) < 3, 1.0, 0.0).astype(BF16)
    qs = []
    for e in heads:
        rest = jnp.full((tq, SB_HEAD_DIM), bias_ref[n_heads * hg + e], F32)
        cols = jnp.zeros((tq, SB_HEAD_DIM), F32)
        for p in range(3):
            part = rest.astype(BF16).astype(F32)
            cols = jnp.where(part_col == p, part, cols)
            rest = rest - part
        qs.append(jnp.concatenate([qall[:, e * SB_HEAD_DIM:(e + 1) * SB_HEAD_DIM], cols.astype(BF16)], axis=1))

    def head_rows(ref, e, j):
        return ref[e * SB_HEAD_DIM:(e + 1) * SB_HEAD_DIM, pl.ds(pl.multiple_of(j * tk, tk), tk)]

    def scores(e, j):
        return _dot(qs[e], jnp.concatenate([head_rows(k_ref, e, j), ones_rows], axis=0))

    def middle(valid, j_next):
        parts = []
        for e in heads:
            log_beta, spm, sp_bf = _sb_softplus(z_scr[e], valid, False)
            parts.append((log_beta, spm, _dot(sp_bf, neg_upper)))
            z_scr[e] = scores(e, j_next)
        for e, (log_beta, spm, neg_later) in zip(heads, parts):
            w_scr[e], carry_scr[e] = _sb_weights(log_beta, spm, neg_later, carry_scr[e], valid)

    def add_values(j):
        for e in heads:
            acc_scr[e] = _sb_pv(w_scr[e], head_rows(v_ref, e, j), acc_scr[e], True)

    for e in heads:
        z_scr[e] = scores(e, qi)
    carry_scr[...] = jnp.zeros(carry_scr.shape, F32)
    acc_scr[...] = jnp.zeros(acc_scr.shape, F32)
    middle(col < row, jnp.maximum(qi - 1, 0))

    def body(jj, _):
        j = qi - 1 - jj
        add_values(j + 1)
        middle(None, jnp.maximum(j - 1, 0))
        return 0

    lax.fori_loop(0, qi, body, 0)
    add_values(0)
    o_ref[...] = jnp.concatenate([acc_scr[e] for e in heads], axis=-1).astype(o_ref.dtype)


def _sb_sample_part(step, n_steps, pt_ref, bias_ref, q_ref, kn_ref, vn_ref, ck_ref, cv_ref, o_ref, kbuf, vbuf,
                    sem, *, t_new, batches_per_step, n_chunks, pages_per_chunk):
    tk = SB_KEY_BLOCK
    n_slots = kbuf.shape[0]
    lookahead = n_slots - 1
    total_chunks = n_steps * batches_per_step * n_chunks
    dsb = q_ref.shape[-1]
    nh = dsb // SB_HEAD_DIM
    rows = nh * t_new

    def page_copies(g, sl):
        bb, cc = g // n_chunks, g % n_chunks
        first = (n_chunks - 1 - cc) * pages_per_chunk
        cps = []
        for p in range(pages_per_chunk):
            page = pt_ref[bb, first + p]
            cps.append(pltpu.make_async_copy(ck_ref.at[page], kbuf.at[sl, p], sem.at[sl, 0]))
            cps.append(pltpu.make_async_copy(cv_ref.at[page], vbuf.at[sl, p], sem.at[sl, 1]))
        return cps

    @pl.when(step == 0)
    def _():
        for g in range(min(lookahead, total_chunks)):
            for cp in page_copies(g, g % n_slots):
                cp.start()

    rrow = lax.broadcasted_iota(jnp.int32, (rows, dsb), 0)
    rcol = lax.broadcasted_iota(jnp.int32, (rows, dsb), 1)
    head_mask = (rrow // t_new) == (rcol // SB_HEAD_DIM)
    bias = bias_ref[:, 0:1]
    neg_upper2 = _neg_strict_upper2(tk)
    pages_per_block = tk // PAGE_SIZE
    n_blocks = pages_per_chunk // pages_per_block
    groups = [range(g, min(g + SAMPLE_GROUP, n_blocks)) for g in range(0, n_blocks, SAMPLE_GROUP)]

    for bb in range(batches_per_step):
        batch = step * batches_per_step + bb
        own_rows = slice(bb * t_new, (bb + 1) * t_new)
        q_rep = jnp.concatenate([q_ref[own_rows, :]] * nh, axis=0)
        qbd = jnp.where(head_mask, q_rep, 0.0).astype(BF16)

        pad = jnp.zeros((128 - t_new, dsb), BF16)
        kb = jnp.concatenate([kn_ref[own_rows, :].astype(BF16), pad], axis=0)
        vb = jnp.concatenate([vn_ref[own_rows, :].astype(BF16), pad], axis=0)
        t_of_row = lax.broadcasted_iota(jnp.int32, (rows, 128), 0) % t_new
        key = lax.broadcasted_iota(jnp.int32, (rows, 128), 1)
        state = _sb_step(qbd, kb, vb, bias, _neg_strict_upper2(128), jnp.zeros((rows, 128), F32),
                         jnp.zeros((rows, dsb), F32), key < t_of_row, False)

        def chunk(c, state, batch=batch, qbd=qbd):
            carry, acc = state
            g = batch * n_chunks + c
            slot = g % n_slots

            @pl.when(g + lookahead < total_chunks)
            def _():
                for cp in page_copies(g + lookahead, (g + lookahead) % n_slots):
                    cp.start()

            for cp in page_copies(g, slot):
                cp.wait()

            def block_of(buf, i):
                first = (n_blocks - 1 - i) * pages_per_block
                pages = [buf[slot, first + p] for p in range(pages_per_block)]
                return jnp.concatenate(pages, axis=1).astype(BF16)

            zs = [_sb_scores(qbd, block_of(kbuf, i), bias, True) for i in groups[0]]
            for gi, group in enumerate(groups):
                parts = [_sb_softplus(z, None, True) for z in zs]
                neg_later = _dot(jnp.concatenate([hilo for _, _, hilo in parts], axis=0), neg_upper2)
                if gi + 1 < len(groups):
                    zs = [_sb_scores(qbd, block_of(kbuf, i), bias, True) for i in groups[gi + 1]]
                for n, (i, (log_beta, spm, _)) in enumerate(zip(group, parts)):
                    wb, carry = _sb_weights(log_beta, spm, neg_later[n * rows:(n + 1) * rows], carry, None)
                    acc = _sb_pv(wb, block_of(vbuf, i), acc, True)
            return carry, acc

        _, acc = lax.fori_loop(0, n_chunks, chunk, state)
        own = jnp.where(head_mask, acc, 0.0).reshape(nh, t_new, dsb)
        o_ref[own_rows, :] = jnp.sum(own, axis=0)


def _sb_fused_kernel(pt_ref, bias_ref, q_ref, k_ref, v_ref, brow_ref, qs_ref, kn_ref, vn_ref, ck_ref, cv_ref,
                     o_ref, os_ref, z_scr, w_scr, carry_scr, acc_scr, kbuf, vbuf, sem, *, tq, n_heads, t_new,
                     batches_per_step, n_chunks, pages_per_chunk):
    hg, qi = pl.program_id(1), pl.program_id(2)
    step = (pl.program_id(0) * pl.num_programs(1) + hg) * pl.num_programs(2) + qi
    n_steps = pl.num_programs(0) * pl.num_programs(1) * pl.num_programs(2)
    _sb_sample_part(step, n_steps, pt_ref, brow_ref, qs_ref, kn_ref, vn_ref, ck_ref, cv_ref, os_ref, kbuf, vbuf,
                    sem, t_new=t_new, batches_per_step=batches_per_step, n_chunks=n_chunks,
                    pages_per_chunk=pages_per_chunk)
    _sb_prompt_part(hg, qi, bias_ref, q_ref, k_ref, v_ref, o_ref, z_scr, w_scr, carry_scr, acc_scr, tq=tq,
                    n_heads=n_heads)


def _sb_fused(q, k_t, v_t, bias, qs, k_new, v_new, cache_k, cache_v, page_table, bias_rows, *, tq,
              heads_per_step, t_new, n_chunks, n_slots):
    bsz, s, dsb = q.shape
    nh = dsb // SB_HEAD_DIM
    assert tq == SB_KEY_BLOCK and s % tq == 0 and nh % heads_per_step == 0
    gw = heads_per_step * SB_HEAD_DIM
    grid = (bsz, nh // heads_per_step, s // tq)
    n_steps = grid[0] * grid[1] * grid[2]
    m = qs.shape[0]
    nbatch = m // t_new
    assert nbatch % n_steps == 0 and page_table.shape[1] % n_chunks == 0
    bps = nbatch // n_steps
    ppc = page_table.shape[1] // n_chunks
    kern = functools.partial(_sb_fused_kernel, tq=tq, n_heads=heads_per_step, t_new=t_new,
                             batches_per_step=bps, n_chunks=n_chunks, pages_per_chunk=ppc)

    def lin(b, hg, qi):
        return (b * grid[1] + hg) * grid[2] + qi

    kv_spec = pl.BlockSpec((None, gw, s), lambda b, hg, qi, pt: (b, hg, 0), pipeline_mode=pl.Buffered(1))
    tile = pl.BlockSpec((None, tq, gw), lambda b, hg, qi, pt: (b, qi, hg))
    srow = pl.BlockSpec((bps * t_new, dsb), lambda b, hg, qi, pt: (lin(b, hg, qi), 0))
    grid_spec = pltpu.PrefetchScalarGridSpec(
        num_scalar_prefetch=1,
        grid=grid,
        in_specs=[pl.BlockSpec(memory_space=pltpu.SMEM), tile, kv_spec, kv_spec,
                  pl.BlockSpec(bias_rows.shape, lambda b, hg, qi, pt: (0, 0)), srow, srow, srow,
                  pl.BlockSpec(memory_space=pl.ANY), pl.BlockSpec(memory_space=pl.ANY)],
        out_specs=[tile, srow],
        scratch_shapes=[pltpu.VMEM((heads_per_step, tq, SB_KEY_BLOCK), F32),
                        pltpu.VMEM((heads_per_step, tq, SB_KEY_BLOCK), BF16),
                        pltpu.VMEM((heads_per_step, tq, 128), F32),
                        pltpu.VMEM((heads_per_step, tq, SB_HEAD_DIM), F32),
                        pltpu.VMEM((n_slots, ppc, dsb, PAGE_SIZE), F32),
                        pltpu.VMEM((n_slots, ppc, dsb, PAGE_SIZE), F32),
                        pltpu.SemaphoreType.DMA((n_slots, 2))])
    return pl.pallas_call(
        kern,
        grid_spec=grid_spec,
        out_shape=[jax.ShapeDtypeStruct((bsz, s, dsb), BF16), jax.ShapeDtypeStruct((m, dsb), F32)],
        compiler_params=pltpu.CompilerParams(
            dimension_semantics=("arbitrary", "arbitrary", "arbitrary"),
            vmem_limit_bytes=VMEM_LIMIT),
        name="sb_fused",
    )(page_table, bias, q, k_t, v_t, bias_rows, qs, k_new, v_new, cache_k, cache_v)


def _mix_xattn_kernel(x_ref, yc_ref, ysb_ref, wo_ref, wq_ref, wxo_ref, mk_ref, mv_ref, g_ref, b_ref,
                      o_ref, *, alpha, n_mem_batches):
    x = x_ref[...]
    dc = yc_ref.shape[-1]
    mix = (_dot(yc_ref[...].astype(BF16), wo_ref[0:dc, :])
           + _dot(ysb_ref[...].astype(BF16), wo_ref[dc:, :]))
    x2 = _layer_norm_rows(alpha * x + mix, g_ref[1:2, :], b_ref[1:2, :])
    q = _dot(x2.astype(BF16), wq_ref[...]).astype(BF16)
    tm, dx = q.shape
    rpb = tm // n_mem_batches
    n_heads = dx // X_HEAD_DIM
    lane_tiles = X_HEAD_DIM // 128

    def mem_head(ref, j, h):
        if ref.shape[-1] == dx:
            return ref[j, :, h * X_HEAD_DIM:(h + 1) * X_HEAD_DIM].astype(BF16)
        per_tok = lane_tiles * n_heads
        parts = [ref[j, pl.ds(t * n_heads + h, ref.shape[1] // per_tok, stride=per_tok), :]
                 for t in range(lane_tiles)]
        return jnp.concatenate(parts, axis=1).astype(BF16)

    pairs = [(j, h) for j in range(n_mem_batches) for h in range(n_heads)]
    scores = [_dot_nt(q[j * rpb:(j + 1) * rpb, h * X_HEAD_DIM:(h + 1) * X_HEAD_DIM], mem_head(mk_ref, j, h))
              * (X_HEAD_DIM ** -0.5) for j, h in pairs]
    probs = []
    for s in scores:
        e = jnp.exp(s - jnp.max(s, axis=-1, keepdims=True))
        probs.append((e / jnp.sum(e, axis=-1, keepdims=True)).astype(BF16))
    outs = [_dot(a, mem_head(mv_ref, j, h)) for a, (j, h) in zip(probs, pairs)]
    rows = [jnp.concatenate(outs[j * n_heads:(j + 1) * n_heads], axis=-1) for j in range(n_mem_batches)]
    o = rows[0] if n_mem_batches == 1 else jnp.concatenate(rows, axis=0)
    xo = _dot(o.astype(BF16), wxo_ref[...])
    o_ref[...] = _layer_norm_rows(alpha * x2 + xo, g_ref[2:3, :], b_ref[2:3, :])


def _mix_xattn(x, yc, ysb, w_out, w_xq, w_xo, mk, mv, g, b, *, alpha, tm, rows_per_mem):
    m, d = x.shape
    nbm = max(tm // rows_per_mem, 1)
    kern = functools.partial(_mix_xattn_kernel, alpha=alpha, n_mem_batches=nbm)
    row = lambda c: pl.BlockSpec((tm, c), lambda i: (i, 0))
    if rows_per_mem >= tm:
        mem_spec = pl.BlockSpec((1,) + mk.shape[1:], lambda i: (i * tm // rows_per_mem, 0, 0))
    else:
        mem_spec = pl.BlockSpec((nbm,) + mk.shape[1:], lambda i: (i, 0, 0))
    return pl.pallas_call(
        kern,
        grid=(m // tm,),
        in_specs=[row(d), row(yc.shape[1]), row(ysb.shape[1]),
                  _resident(w_out.shape), _resident(w_xq.shape), _resident(w_xo.shape),
                  mem_spec, mem_spec, _resident(g.shape), _resident(b.shape)],
        out_specs=row(d),
        out_shape=jax.ShapeDtypeStruct((m, d), F32),
        compiler_params=pltpu.CompilerParams(dimension_semantics=("arbitrary",),
                                             vmem_limit_bytes=VMEM_LIMIT),
        name="mix_xattn",
    )(x, yc, ysb, w_out, w_xq, w_xo, mk, mv, g, b)


def _tile(m, want):
    return want if m % want == 0 else m


def kernel(x_prompt, x_sample, mem_prompt, cache_k, cache_v, state_conv, cache_mem_k, cache_mem_v, page_table, ln_g, ln_b, w_ffn1_gu, w_ffn1_down, w_in, conv_w, sb_bias, w_out, w_xq, w_xk, w_xv, w_xo, w_ffn2_gu, w_ffn2_down):
    depth = ln_g.shape[0]
    alpha = (2 * depth) ** 0.25
    bp, s, d = x_prompt.shape
    db, t_new, _ = x_sample.shape
    n_mem = mem_prompt.shape[1]
    dc = conv_w.shape[-1]
    dsb = (w_in.shape[-1] - 3 * dc) // 3
    nh = dsb // SB_HEAD_DIM
    d_ff = w_ffn1_down.shape[1]
    page = cache_k.shape[2]

    yp = x_prompt.reshape(bp * s, d)
    ys = x_sample.reshape(db * t_new, d)
    outs = {n: [] for n in ("kp", "vp", "cp", "mkp", "mvp", "ks", "vs", "cs")}
    for l in range(depth):
        g, b = ln_g[l], ln_b[l]
        w1gu, w1dn = w_ffn1_gu[l].astype(BF16), w_ffn1_down[l].astype(BF16)
        w2gu, w2dn = w_ffn2_gu[l].astype(BF16), w_ffn2_down[l].astype(BF16)
        win, wout = w_in[l].astype(BF16), w_out[l].astype(BF16)
        wxq, wxk, wxv, wxo = (w[l].astype(BF16) for w in (w_xq, w_xk, w_xv, w_xo))
        cw = conv_w[l]
        bias = sb_bias[l].astype(F32)
        ffn = functools.partial(_ffn_ln, alpha=alpha)

        mk, mv = _memproj(mem_prompt.reshape(bp * n_mem, d), wxk, wxv)
        x1p = ffn(yp, w1gu, w1dn, g[0:1], b[0:1], tm=_tile(bp * s, ROW_TILE))
        ycp, qp, kt, vt, kt_bf, vt_bf, cst = _inproj_prompt(
            x1p.reshape(bp, s, d), win[:, :3 * dc + dsb], win[:, 3 * dc + dsb:].T, cw, tm=_tile(s, ROW_TILE))
        ms = db * t_new
        x1 = ffn(ys, w1gu, w1dn, g[0:1], b[0:1], tm=_tile(ms, SAMPLE_ROW_TILE))
        c0 = jnp.repeat(state_conv[l][:, 0, :], t_new, axis=0)
        c1 = jnp.repeat(state_conv[l][:, 1, :], t_new, axis=0)
        yc, q, k, v, u = _inproj_sample(x1, win, cw, c0, c1, seq_rows=t_new, tm=_tile(ms, SAMPLE_ROW_TILE))
        bias_rows = jnp.broadcast_to(jnp.repeat(bias, t_new)[:, None], (nh * t_new, 128))
        ck_t = cache_k[l].transpose(0, 2, 3, 1).reshape(-1, dsb, page)
        cv_t = cache_v[l].transpose(0, 2, 3, 1).reshape(-1, dsb, page)
        ysbp, ysb = _sb_fused(qp, kt_bf, vt_bf, bias, q, k, v, ck_t, cv_t, page_table, bias_rows,
                              tq=_tile(s, SB_KEY_BLOCK), heads_per_step=min(nh, 4), t_new=t_new,
                              n_chunks=SAMPLE_CHUNKS, n_slots=SAMPLE_SLOTS)

        x3 = _mix_xattn(x1p, ycp.reshape(bp * s, dc), ysbp.reshape(bp * s, dsb), wout, wxq, wxo,
                        mk.reshape(bp, n_mem, -1), mv.reshape(bp, n_mem, -1), g, b,
                        alpha=alpha, tm=_tile(s, ROW_TILE), rows_per_mem=s)
        yp = ffn(x3, w2gu, w2dn, g[3:4], b[3:4], tm=_tile(bp * s, ROW_TILE))
        outs["kp"].append(kt.reshape(bp, nh, SB_HEAD_DIM, s).transpose(0, 3, 1, 2))
        outs["vp"].append(vt.reshape(bp, nh, SB_HEAD_DIM, s).transpose(0, 3, 1, 2))
        outs["cp"].append(cst)
        outs["mkp"].append(mk.reshape(bp, n_mem, -1, X_HEAD_DIM))
        outs["mvp"].append(mv.reshape(bp, n_mem, -1, X_HEAD_DIM))

        mem_per_step = 8

        def mem_rows(c):
            nxh = c.shape[2]
            c = c.reshape(db, n_mem, nxh, X_HEAD_DIM // 128, 128).transpose(0, 1, 3, 2, 4)
            return c.reshape(db, n_mem * nxh * (X_HEAD_DIM // 128), 128)

        x3 = _mix_xattn(x1, yc, ysb, wout, wxq, wxo, mem_rows(cache_mem_k[l]), mem_rows(cache_mem_v[l]),
                        g, b, alpha=alpha, tm=_tile(ms, mem_per_step * t_new), rows_per_mem=t_new)
        ys = ffn(x3, w2gu, w2dn, g[3:4], b[3:4], tm=_tile(ms, SAMPLE_ROW_TILE))
        outs["ks"].append(k.reshape(db, t_new, nh, SB_HEAD_DIM))
        outs["vs"].append(v.reshape(db, t_new, nh, SB_HEAD_DIM))
        outs["cs"].append(u.reshape(db, t_new, dc)[:, t_new - (CONV_WIDTH - 1):, :])

    st = lambda n: jnp.stack(outs[n])
    return (yp.reshape(bp, s, d), ys.reshape(db, t_new, d), st("kp"), st("vp"), st("cp"), st("mkp"),
            st("mvp"), st("ks"), st("vs"), st("cs"))
```

```python
import functools

import jax
import jax.numpy as jnp
from jax import lax
from jax.experimental import pallas as pl
from jax.experimental.pallas import tpu as pltpu

F32 = jnp.float32
BF16 = jnp.bfloat16

LN_EPS = 1e-5
NEG_LOG2_E = -1.4426950408889634
SB_HEAD_DIM = 64
X_HEAD_DIM = 256
CONV_WIDTH = 3
PAGE_SIZE = 128
MXU_WIDTH = 256
SB_KEY_BLOCK = MXU_WIDTH
SAMPLE_GROUP = 4
SAMPLE_CHUNKS = 4
SAMPLE_SLOTS = 4
ROW_TILE = 1024
SAMPLE_ROW_TILE = 512
VMEM_LIMIT = 56 * 1024 * 1024


def _dot(a, b):
    return jnp.dot(a, b, preferred_element_type=F32)


def _dot_nt(a, b):
    return lax.dot_general(a, b, (((1,), (1,)), ((), ())), preferred_element_type=F32)


def _layer_norm_rows(y, g, b):
    mu = jnp.mean(y, axis=-1, keepdims=True)
    d = y - mu
    var = jnp.mean(d * d, axis=-1, keepdims=True)
    return d * lax.rsqrt(var + LN_EPS) * g + b


def _resident(shape):
    nd = len(shape)
    return pl.BlockSpec(shape, lambda *_: (0,) * nd, pipeline_mode=pl.Buffered(1))


def _ffn_chunks(d_ff):
    if d_ff % MXU_WIDTH:
        return ((0, d_ff),)
    first = (d_ff // MXU_WIDTH + 1) // 2 * MXU_WIDTH
    return ((0, first), (first, d_ff)) if first < d_ff else ((0, d_ff),)


def _ffn_ln_kernel(x_ref, wgu_ref, wdn_ref, g_ref, b_ref, o_ref, *, d_ff, alpha):
    x = x_ref[...]
    xb = x.astype(BF16)
    acc = jnp.zeros(x.shape, F32)
    for lo, hi in _ffn_chunks(d_ff):
        gate = _dot(xb, wgu_ref[:, lo:hi])
        up = _dot(xb, wgu_ref[:, d_ff + lo:d_ff + hi])
        h = (gate / (1.0 + jnp.exp(-gate))) * up
        acc = acc + _dot(h.astype(BF16), wdn_ref[lo:hi, :])
    y = alpha * x + 0.5 * acc
    o_ref[...] = _layer_norm_rows(y, g_ref[...], b_ref[...])


def _ffn_ln(x, wgu, wdn, g, b, *, alpha, tm):
    m, d = x.shape
    d_ff = wdn.shape[0]
    kern = functools.partial(_ffn_ln_kernel, d_ff=d_ff, alpha=alpha)
    return pl.pallas_call(
        kern,
        grid=(m // tm,),
        in_specs=[pl.BlockSpec((tm, d), lambda i: (i, 0)),
                  _resident(wgu.shape), _resident(wdn.shape),
                  _resident(g.shape), _resident(b.shape)],
        out_specs=pl.BlockSpec((tm, d), lambda i: (i, 0)),
        out_shape=jax.ShapeDtypeStruct((m, d), F32),
        compiler_params=pltpu.CompilerParams(dimension_semantics=("arbitrary",),
                                             vmem_limit_bytes=VMEM_LIMIT),
        name="ffn_ln",
    )(x, wgu, wdn, g, b)


def _memproj_kernel(x_ref, wk_ref, wv_ref, k_ref, v_ref):
    xb = x_ref[...].astype(BF16)
    k_ref[...] = _dot(xb, wk_ref[...])
    v_ref[...] = _dot(xb, wv_ref[...])


def _memproj(x, wk, wv):
    m, d = x.shape
    n = wk.shape[1]
    return pl.pallas_call(
        _memproj_kernel,
        grid=(1,),
        in_specs=[_resident(x.shape), _resident(wk.shape), _resident(wv.shape)],
        out_specs=[pl.BlockSpec((m, n), lambda i: (0, 0)), pl.BlockSpec((m, n), lambda i: (0, 0))],
        out_shape=[jax.ShapeDtypeStruct((m, n), F32), jax.ShapeDtypeStruct((m, n), F32)],
        compiler_params=pltpu.CompilerParams(vmem_limit_bytes=VMEM_LIMIT),
        name="memproj",
    )(x, wk, wv)


def _conv_taps(u, prev2, prev1, cw_ref, seq_rows):
    r = u.shape[0]
    row = lax.broadcasted_iota(jnp.int32, (r, 1), 0)
    t = row if seq_rows is None else row % seq_rows
    r1 = pltpu.roll(u, 1, axis=0)
    r2 = pltpu.roll(u, 2, axis=0)
    back1 = jnp.where(t >= 1, r1, prev1)
    back2 = jnp.where(t >= 2, r2, jnp.where(t == 1, prev1, prev2))
    return cw_ref[0:1, :] * back2 + cw_ref[1:2, :] * back1 + cw_ref[2:3, :] * u


def _inproj_prompt_kernel(x_ref, w_ref, wkvt_ref, cw_ref, yc_ref, q_ref, kt_ref, vt_ref, ktb_ref, vtb_ref,
                          st_ref, carry_ref, *, dc, dsb):
    i = pl.program_id(1)

    @pl.when(i == 0)
    def _():
        carry_ref[...] = jnp.zeros(carry_ref.shape, F32)

    xb = x_ref[...].astype(BF16)
    p = _dot(xb, w_ref[...])
    gb, gc, hx = p[:, 0:dc], p[:, dc:2 * dc], p[:, 2 * dc:3 * dc]
    q = p[:, 3 * dc:3 * dc + dsb]
    u = gc * hx
    prev = carry_ref[...]
    z = _conv_taps(u, prev[6:7, :], prev[7:8, :], cw_ref, None)
    yc_ref[...] = (gb * z).astype(yc_ref.dtype)
    tail = u[u.shape[0] - 8:, :]
    carry_ref[...] = tail
    st_ref[...] = tail[6:8, :]
    q_ref[...] = (q * (SB_HEAD_DIM ** -0.5)).astype(q_ref.dtype)
    kvt = _dot_nt(wkvt_ref[...], xb)
    kt, vt = kvt[0:dsb, :], kvt[dsb:, :]
    kt_ref[...] = kt
    vt_ref[...] = vt
    ktb_ref[...] = kt.astype(BF16)
    vtb_ref[...] = vt.astype(BF16)


def _inproj_prompt(x, w_main, w_kv_t, conv_w, *, tm):
    bsz, s, d = x.shape
    dc = conv_w.shape[1]
    dsb = w_kv_t.shape[0] // 2
    kern = functools.partial(_inproj_prompt_kernel, dc=dc, dsb=dsb)
    row = lambda c: pl.BlockSpec((None, tm, c), lambda b, i: (b, i, 0))
    colm = pl.BlockSpec((None, dsb, tm), lambda b, i: (b, 0, i))
    return pl.pallas_call(
        kern,
        grid=(bsz, s // tm),
        in_specs=[row(d), _resident(w_main.shape), _resident(w_kv_t.shape), _resident(conv_w.shape)],
        out_specs=[row(dc), row(dsb), colm, colm, colm, colm,
                   pl.BlockSpec((None, CONV_WIDTH - 1, dc), lambda b, i: (b, 0, 0))],
        out_shape=[jax.ShapeDtypeStruct((bsz, s, dc), BF16),
                   jax.ShapeDtypeStruct((bsz, s, dsb), BF16),
                   jax.ShapeDtypeStruct((bsz, dsb, s), F32),
                   jax.ShapeDtypeStruct((bsz, dsb, s), F32),
                   jax.ShapeDtypeStruct((bsz, dsb, s), BF16),
                   jax.ShapeDtypeStruct((bsz, dsb, s), BF16),
                   jax.ShapeDtypeStruct((bsz, CONV_WIDTH - 1, dc), F32)],
        scratch_shapes=[pltpu.VMEM((8, dc), F32)],
        compiler_params=pltpu.CompilerParams(dimension_semantics=("arbitrary", "arbitrary"),
                                             vmem_limit_bytes=VMEM_LIMIT),
        name="inproj_prompt",
    )(x, w_main, w_kv_t, conv_w)


def _inproj_sample_kernel(x_ref, w_ref, cw_ref, c0_ref, c1_ref, yc_ref, q_ref, k_ref, v_ref, u_ref,
                          *, dc, dsb, seq_rows):
    p = _dot(x_ref[...].astype(BF16), w_ref[...])
    gb, gc, hx = p[:, 0:dc], p[:, dc:2 * dc], p[:, 2 * dc:3 * dc]
    o = 3 * dc
    u = gc * hx
    z = _conv_taps(u, c0_ref[...], c1_ref[...], cw_ref, seq_rows)
    yc_ref[...] = (gb * z).astype(yc_ref.dtype)
    u_ref[...] = u
    q_ref[...] = p[:, o:o + dsb] * (SB_HEAD_DIM ** -0.5)
    k_ref[...] = p[:, o + dsb:o + 2 * dsb]
    v_ref[...] = p[:, o + 2 * dsb:o + 3 * dsb]


def _inproj_sample(x, w_in, conv_w, c0, c1, *, seq_rows, tm):
    m, d = x.shape
    dc = conv_w.shape[1]
    dsb = (w_in.shape[1] - 3 * dc) // 3
    kern = functools.partial(_inproj_sample_kernel, dc=dc, dsb=dsb, seq_rows=seq_rows)
    row = lambda c: pl.BlockSpec((tm, c), lambda i: (i, 0))
    return pl.pallas_call(
        kern,
        grid=(m // tm,),
        in_specs=[row(d), _resident(w_in.shape), _resident(conv_w.shape), row(dc), row(dc)],
        out_specs=[row(dc), row(dsb), row(dsb), row(dsb), row(dc)],
        out_shape=[jax.ShapeDtypeStruct((m, dc), BF16),
                   jax.ShapeDtypeStruct((m, dsb), F32),
                   jax.ShapeDtypeStruct((m, dsb), F32),
                   jax.ShapeDtypeStruct((m, dsb), F32),
                   jax.ShapeDtypeStruct((m, dc), F32)],
        compiler_params=pltpu.CompilerParams(dimension_semantics=("arbitrary",),
                                             vmem_limit_bytes=VMEM_LIMIT),
        name="inproj_sample",
    )(x, w_in, conv_w, c0, c1)


def _neg_strict_upper(n, copies):
    r = lax.broadcasted_iota(jnp.int32, (copies * n, n), 0)
    c = lax.broadcasted_iota(jnp.int32, (copies * n, n), 1)
    for _ in range(copies - 1):
        r = jnp.where(r >= n, r - n, r)
    return jnp.where(r > c, -1.0, 0.0).astype(BF16)


def _neg_strict_upper2(n):
    return _neg_strict_upper(n, 2)


def _sb_scores(qb, kb, bias, transposed):
    return (_dot(qb, kb) if transposed else _dot_nt(qb, kb)) + bias


def _sb_softplus(z, valid, split):
    sp = jnp.maximum(z, 0.0) + jnp.log(1.0 + jnp.exp2(jnp.abs(z) * NEG_LOG2_E))
    spm = sp if valid is None else jnp.where(valid, sp, 0.0)
    hi = spm.astype(BF16)
    if not split:
        return z - sp, spm, hi
    lo = (spm - hi.astype(F32)).astype(BF16)
    return z - sp, spm, jnp.concatenate([hi, lo], axis=1)


def _sb_weights(log_beta, spm, neg_later, carry, valid):
    w = jnp.exp(log_beta + neg_later + jnp.tile(carry, (1, log_beta.shape[1] // carry.shape[1])))
    if valid is not None:
        w = jnp.where(valid, w, 0.0)
    return w.astype(BF16), carry - jnp.sum(spm, axis=-1, keepdims=True)


def _sb_pv(wb, vb, acc, transposed):
    return acc + (_dot_nt(wb, vb) if transposed else _dot(wb, vb))


def _sb_step(qb, kb, vb, bias, neg_upper2, carry, acc, valid, transposed):
    log_beta, spm, hilo = _sb_softplus(_sb_scores(qb, kb, bias, transposed), valid, True)
    wb, carry = _sb_weights(log_beta, spm, _dot(hilo, neg_upper2), carry, valid)
    return carry, _sb_pv(wb, vb, acc, transposed)


def _sb_prompt_part(hg, qi, bias_ref, q_ref, k_ref, v_ref, o_ref, z_scr, w_scr, carry_scr, acc_scr, *, tq,
                    n_heads):
    tk = SB_KEY_BLOCK
    neg_upper = _neg_strict_upper(tk, 1)
    qall = q_ref[...]
    heads = tuple(range(n_heads))
    row = lax.broadcasted_iota(jnp.int32, (tq, tk), 0)
    col = lax.broadcasted_iota(jnp.int32, (tq, tk), 1)

    part_col = lax.broadcasted_iota(jnp.int32, (tq, SB_HEAD_DIM), 1)
    ones_rows = jnp.where(lax.broadcasted_iota(jnp.int32, (SB_HEAD_DIM, tk), 0) < 3, 1.0, 0.0).astype(BF16)
    qs = []
    for e in heads:
        rest = jnp.full((tq, SB_HEAD_DIM), bias_ref[n_heads * hg + e], F32)
        cols = jnp.zeros((tq, SB_HEAD_DIM), F32)
        for p in range(3):
            part = rest.astype(BF16).astype(F32)
            cols = jnp.where(part_col == p, part, cols)
            rest = rest - part
        qs.append(jnp.concatenate([qall[:, e * SB_HEAD_DIM:(e + 1) * SB_HEAD_DIM], cols.astype(BF16)], axis=1))

    def head_rows(ref, e, j):
        return ref[e * SB_HEAD_DIM:(e + 1) * SB_HEAD_DIM, pl.ds(pl.multiple_of(j * tk, tk), tk)]

    def scores(e, j):
        return _dot(qs[e], jnp.concatenate([head_rows(k_ref, e, j), ones_rows], axis=0))

    def middle(valid, j_next):
        parts = []
        for e in heads:
            log_beta, spm, sp_bf = _sb_softplus(z_scr[e], valid, False)
            parts.append((log_beta, spm, _dot(sp_bf, neg_upper)))
            z_scr[e] = scores(e, j_next)
        for e, (log_beta, spm, neg_later) in zip(heads, parts):
            w_scr[e], carry_scr[e] = _sb_weights(log_beta, spm, neg_later, carry_scr[e], valid)

    def add_values(j):
        for e in heads:
            acc_scr[e] = _sb_pv(w_scr[e], head_rows(v_ref, e, j), acc_scr[e], True)

    for e in heads:
        z_scr[e] = scores(e, qi)
    carry_scr[...] = jnp.zeros(carry_scr.shape, F32)
    acc_scr[...] = jnp.zeros(acc_scr.shape, F32)
    middle(col < row, jnp.maximum(qi - 1, 0))

    def older_blocks():
        def body(jj, _):
            j = qi - 1 - jj
            add_values(j + 1)
            middle(None, jnp.maximum(j - 1, 0))
            return 0

        lax.fori_loop(0, qi, body, 0)
        add_values(0)
        o_ref[...] = jnp.concatenate([acc_scr[e] for e in heads], axis=-1).astype(o_ref.dtype)

    return older_blocks


def _sb_sample_part(step, n_steps, pt_ref, bias_ref, q_ref, kn_ref, vn_ref, ck_ref, cv_ref, o_ref, kbuf, vbuf,
                    sem, *, t_new, batches_per_step, n_chunks, pages_per_chunk, after_first_new_keys):
    tk = SB_KEY_BLOCK
    n_slots = kbuf.shape[0]
    lookahead = n_slots - 1
    total_chunks = n_steps * batches_per_step * n_chunks
    dsb = q_ref.shape[-1]
    nh = dsb // SB_HEAD_DIM
    rows = nh * t_new

    def page_copies(g, sl):
        bb, cc = g // n_chunks, g % n_chunks
        first = (n_chunks - 1 - cc) * pages_per_chunk
        cps = []
        for p in range(pages_per_chunk):
            page = pt_ref[bb, first + p]
            cps.append(pltpu.make_async_copy(ck_ref.at[page], kbuf.at[sl, p], sem.at[sl, 0]))
            cps.append(pltpu.make_async_copy(cv_ref.at[page], vbuf.at[sl, p], sem.at[sl, 1]))
        return cps

    @pl.when(step == 0)
    def _():
        for g in range(min(lookahead, total_chunks)):
            for cp in page_copies(g, g % n_slots):
                cp.start()

    rrow = lax.broadcasted_iota(jnp.int32, (rows, dsb), 0)
    rcol = lax.broadcasted_iota(jnp.int32, (rows, dsb), 1)
    head_mask = (rrow // t_new) == (rcol // SB_HEAD_DIM)
    bias = bias_ref[:, 0:1]
    neg_upper2 = _neg_strict_upper2(tk)
    pages_per_block = tk // PAGE_SIZE
    n_blocks = pages_per_chunk // pages_per_block
    groups = [range(g, min(g + SAMPLE_GROUP, n_blocks)) for g in range(0, n_blocks, SAMPLE_GROUP)]

    for bb in range(batches_per_step):
        batch = step * batches_per_step + bb
        own_rows = slice(bb * t_new, (bb + 1) * t_new)
        q_rep = jnp.concatenate([q_ref[own_rows, :]] * nh, axis=0)
        qbd = jnp.where(head_mask, q_rep, 0.0).astype(BF16)

        pad = jnp.zeros((128 - t_new, dsb), BF16)
        kb = jnp.concatenate([kn_ref[own_rows, :].astype(BF16), pad], axis=0)
        vb = jnp.concatenate([vn_ref[own_rows, :].astype(BF16), pad], axis=0)
        t_of_row = lax.broadcasted_iota(jnp.int32, (rows, 128), 0) % t_new
        key = lax.broadcasted_iota(jnp.int32, (rows, 128), 1)
        state = _sb_step(qbd, kb, vb, bias, _neg_strict_upper2(128), jnp.zeros((rows, 128), F32),
                         jnp.zeros((rows, dsb), F32), key < t_of_row, False)
        if bb == 0:
            after_first_new_keys()

        def chunk(c, state, batch=batch, qbd=qbd):
            carry, acc = state
            g = batch * n_chunks + c
            slot = g % n_slots

            @pl.when(g + lookahead < total_chunks)
            def _():
                for cp in page_copies(g + lookahead, (g + lookahead) % n_slots):
                    cp.start()

            for cp in page_copies(g, slot):
                cp.wait()

            def block_of(buf, i):
                first = (n_blocks - 1 - i) * pages_per_block
                pages = [buf[slot, first + p] for p in range(pages_per_block)]
                return jnp.concatenate(pages, axis=1).astype(BF16)

            zs = [_sb_scores(qbd, block_of(kbuf, i), bias, True) for i in groups[0]]
            for gi, group in enumerate(groups):
                parts = [_sb_softplus(z, None, True) for z in zs]
                neg_later = _dot(jnp.concatenate([hilo for _, _, hilo in parts], axis=0), neg_upper2)
                if gi + 1 < len(groups):
                    zs = [_sb_scores(qbd, block_of(kbuf, i), bias, True) for i in groups[gi + 1]]
                for n, (i, (log_beta, spm, _)) in enumerate(zip(group, parts)):
                    wb, carry = _sb_weights(log_beta, spm, neg_later[n * rows:(n + 1) * rows], carry, None)
                    acc = _sb_pv(wb, block_of(vbuf, i), acc, True)
            return carry, acc

        _, acc = lax.fori_loop(0, n_chunks, chunk, state)
        own = jnp.where(head_mask, acc, 0.0).reshape(nh, t_new, dsb)
        o_ref[own_rows, :] = jnp.sum(own, axis=0)


def _sb_fused_kernel(pt_ref, bias_ref, q_ref, k_ref, v_ref, brow_ref, qs_ref, kn_ref, vn_ref, ck_ref, cv_ref,
                     o_ref, os_ref, z_scr, w_scr, carry_scr, acc_scr, kbuf, vbuf, sem, *, tq, n_heads, t_new,
                     batches_per_step, n_chunks, pages_per_chunk):
    hg, qi = pl.program_id(1), pl.program_id(2)
    step = (pl.program_id(0) * pl.num_programs(1) + hg) * pl.num_programs(2) + qi
    n_steps = pl.num_programs(0) * pl.num_programs(1) * pl.num_programs(2)
    prompt_rest = []

    def prompt_first_block():
        prompt_rest.append(_sb_prompt_part(hg, qi, bias_ref, q_ref, k_ref, v_ref, o_ref, z_scr, w_scr,
                                           carry_scr, acc_scr, tq=tq, n_heads=n_heads))

    _sb_sample_part(step, n_steps, pt_ref, brow_ref, qs_ref, kn_ref, vn_ref, ck_ref, cv_ref, os_ref, kbuf, vbuf,
                    sem, t_new=t_new, batches_per_step=batches_per_step, n_chunks=n_chunks,
                    pages_per_chunk=pages_per_chunk, after_first_new_keys=prompt_first_block)
    prompt_rest[0]()


def _sb_fused(q, k_t, v_t, bias, qs, k_new, v_new, cache_k, cache_v, page_table, bias_rows, *, tq,
              heads_per_step, t_new, n_chunks, n_slots):
    bsz, s, dsb = q.shape
    nh = dsb // SB_HEAD_DIM
    assert tq == SB_KEY_BLOCK and s % tq == 0 and nh % heads_per_step == 0
    gw = heads_per_step * SB_HEAD_DIM
    grid = (bsz, nh // heads_per_step, s // tq)
    n_steps = grid[0] * grid[1] * grid[2]
    m = qs.shape[0]
    nbatch = m // t_new
    assert nbatch % n_steps == 0 and page_table.shape[1] % n_chunks == 0
    bps = nbatch // n_steps
    ppc = page_table.shape[1] // n_chunks
    kern = functools.partial(_sb_fused_kernel, tq=tq, n_heads=heads_per_step, t_new=t_new,
                             batches_per_step=bps, n_chunks=n_chunks, pages_per_chunk=ppc)

    def lin(b, hg, qi):
        return (b * grid[1] + hg) * grid[2] + qi

    kv_spec = pl.BlockSpec((None, gw, s), lambda b, hg, qi, pt: (b, hg, 0), pipeline_mode=pl.Buffered(1))
    tile = pl.BlockSpec((None, tq, gw), lambda b, hg, qi, pt: (b, qi, hg))
    srow = pl.BlockSpec((bps * t_new, dsb), lambda b, hg, qi, pt: (lin(b, hg, qi), 0))
    grid_spec = pltpu.PrefetchScalarGridSpec(
        num_scalar_prefetch=1,
        grid=grid,
        in_specs=[pl.BlockSpec(memory_space=pltpu.SMEM), tile, kv_spec, kv_spec,
                  pl.BlockSpec(bias_rows.shape, lambda b, hg, qi, pt: (0, 0)), srow, srow, srow,
                  pl.BlockSpec(memory_space=pl.ANY), pl.BlockSpec(memory_space=pl.ANY)],
        out_specs=[tile, srow],
        scratch_shapes=[pltpu.VMEM((heads_per_step, tq, SB_KEY_BLOCK), F32),
                        pltpu.VMEM((heads_per_step, tq, SB_KEY_BLOCK), BF16),
                        pltpu.VMEM((heads_per_step, tq, 128), F32),
                        pltpu.VMEM((heads_per_step, tq, SB_HEAD_DIM), F32),
                        pltpu.VMEM((n_slots, ppc, dsb, PAGE_SIZE), F32),
                        pltpu.VMEM((n_slots, ppc, dsb, PAGE_SIZE), F32),
                        pltpu.SemaphoreType.DMA((n_slots, 2))])
    return pl.pallas_call(
        kern,
        grid_spec=grid_spec,
        out_shape=[jax.ShapeDtypeStruct((bsz, s, dsb), BF16), jax.ShapeDtypeStruct((m, dsb), F32)],
        compiler_params=pltpu.CompilerParams(
            dimension_semantics=("arbitrary", "arbitrary", "arbitrary"),
            vmem_limit_bytes=VMEM_LIMIT),
        name="sb_fused",
    )(page_table, bias, q, k_t, v_t, bias_rows, qs, k_new, v_new, cache_k, cache_v)


def _mix_xattn_kernel(x_ref, yc_ref, ysb_ref, wo_ref, wq_ref, wxo_ref, mk_ref, mv_ref, g_ref, b_ref,
                      o_ref, *, alpha, n_mem_batches):
    x = x_ref[...]
    dc = yc_ref.shape[-1]
    mix = (_dot(yc_ref[...].astype(BF16), wo_ref[0:dc, :])
           + _dot(ysb_ref[...].astype(BF16), wo_ref[dc:, :]))
    x2 = _layer_norm_rows(alpha * x + mix, g_ref[1:2, :], b_ref[1:2, :])
    q = _dot(x2.astype(BF16), wq_ref[...]).astype(BF16)
    tm, dx = q.shape
    rpb = tm // n_mem_batches
    n_heads = dx // X_HEAD_DIM
    lane_tiles = X_HEAD_DIM // 128

    def mem_head(ref, j, h):
        if ref.shape[-1] == dx:
            return ref[j, :, h * X_HEAD_DIM:(h + 1) * X_HEAD_DIM].astype(BF16)
        per_tok = lane_tiles * n_heads
        parts = [ref[j, pl.ds(t * n_heads + h, ref.shape[1] // per_tok, stride=per_tok), :]
                 for t in range(lane_tiles)]
        return jnp.concatenate(parts, axis=1).astype(BF16)

    pairs = [(j, h) for j in range(n_mem_batches) for h in range(n_heads)]
    scores = [_dot_nt(q[j * rpb:(j + 1) * rpb, h * X_HEAD_DIM:(h + 1) * X_HEAD_DIM], mem_head(mk_ref, j, h))
              * (X_HEAD_DIM ** -0.5) for j, h in pairs]
    probs = []
    for s in scores:
        e = jnp.exp(s - jnp.max(s, axis=-1, keepdims=True))
        probs.append((e / jnp.sum(e, axis=-1, keepdims=True)).astype(BF16))
    outs = [_dot(a, mem_head(mv_ref, j, h)) for a, (j, h) in zip(probs, pairs)]
    rows = [jnp.concatenate(outs[j * n_heads:(j + 1) * n_heads], axis=-1) for j in range(n_mem_batches)]
    o = rows[0] if n_mem_batches == 1 else jnp.concatenate(rows, axis=0)
    xo = _dot(o.astype(BF16), wxo_ref[...])
    o_ref[...] = _layer_norm_rows(alpha * x2 + xo, g_ref[2:3, :], b_ref[2:3, :])


def _mix_xattn(x, yc, ysb, w_out, w_xq, w_xo, mk, mv, g, b, *, alpha, tm, rows_per_mem):
    m, d = x.shape
    nbm = max(tm // rows_per_mem, 1)
    kern = functools.partial(_mix_xattn_kernel, alpha=alpha, n_mem_batches=nbm)
    row = lambda c: pl.BlockSpec((tm, c), lambda i: (i, 0))
    if rows_per_mem >= tm:
        mem_spec = pl.BlockSpec((1,) + mk.shape[1:], lambda i: (i * tm // rows_per_mem, 0, 0))
    else:
        mem_spec = pl.BlockSpec((nbm,) + mk.shape[1:], lambda i: (i, 0, 0))
    return pl.pallas_call(
        kern,
        grid=(m // tm,),
        in_specs=[row(d), row(yc.shape[1]), row(ysb.shape[1]),
                  _resident(w_out.shape), _resident(w_xq.shape), _resident(w_xo.shape),
                  mem_spec, mem_spec, _resident(g.shape), _resident(b.shape)],
        out_specs=row(d),
        out_shape=jax.ShapeDtypeStruct((m, d), F32),
        compiler_params=pltpu.CompilerParams(dimension_semantics=("arbitrary",),
                                             vmem_limit_bytes=VMEM_LIMIT),
        name="mix_xattn",
    )(x, yc, ysb, w_out, w_xq, w_xo, mk, mv, g, b)


def _tile(m, want):
    return want if m % want == 0 else m


def kernel(x_prompt, x_sample, mem_prompt, cache_k, cache_v, state_conv, cache_mem_k, cache_mem_v, page_table, ln_g, ln_b, w_ffn1_gu, w_ffn1_down, w_in, conv_w, sb_bias, w_out, w_xq, w_xk, w_xv, w_xo, w_ffn2_gu, w_ffn2_down):
    depth = ln_g.shape[0]
    alpha = (2 * depth) ** 0.25
    bp, s, d = x_prompt.shape
    db, t_new, _ = x_sample.shape
    n_mem = mem_prompt.shape[1]
    dc = conv_w.shape[-1]
    dsb = (w_in.shape[-1] - 3 * dc) // 3
    nh = dsb // SB_HEAD_DIM
    d_ff = w_ffn1_down.shape[1]
    page = cache_k.shape[2]

    yp = x_prompt.reshape(bp * s, d)
    ys = x_sample.reshape(db * t_new, d)
    outs = {n: [] for n in ("kp", "vp", "cp", "mkp", "mvp", "ks", "vs", "cs")}
    for l in range(depth):
        g, b = ln_g[l], ln_b[l]
        w1gu, w1dn = w_ffn1_gu[l].astype(BF16), w_ffn1_down[l].astype(BF16)
        w2gu, w2dn = w_ffn2_gu[l].astype(BF16), w_ffn2_down[l].astype(BF16)
        win, wout = w_in[l].astype(BF16), w_out[l].astype(BF16)
        wxq, wxk, wxv, wxo = (w[l].astype(BF16) for w in (w_xq, w_xk, w_xv, w_xo))
        cw = conv_w[l]
        bias = sb_bias[l].astype(F32)
        ffn = functools.partial(_ffn_ln, alpha=alpha)

        mk, mv = _memproj(mem_prompt.reshape(bp * n_mem, d), wxk, wxv)
        x1p = ffn(yp, w1gu, w1dn, g[0:1], b[0:1], tm=_tile(bp * s, ROW_TILE))
        ycp, qp, kt, vt, kt_bf, vt_bf, cst = _inproj_prompt(
            x1p.reshape(bp, s, d), win[:, :3 * dc + dsb], win[:, 3 * dc + dsb:].T, cw, tm=_tile(s, ROW_TILE))
        ms = db * t_new
        x1 = ffn(ys, w1gu, w1dn, g[0:1], b[0:1], tm=_tile(ms, SAMPLE_ROW_TILE))
        c0 = jnp.repeat(state_conv[l][:, 0, :], t_new, axis=0)
        c1 = jnp.repeat(state_conv[l][:, 1, :], t_new, axis=0)
        yc, q, k, v, u = _inproj_sample(x1, win, cw, c0, c1, seq_rows=t_new, tm=_tile(ms, SAMPLE_ROW_TILE))
        bias_rows = jnp.broadcast_to(jnp.repeat(bias, t_new)[:, None], (nh * t_new, 128))
        ck_t = cache_k[l].transpose(0, 2, 3, 1).reshape(-1, dsb, page)
        cv_t = cache_v[l].transpose(0, 2, 3, 1).reshape(-1, dsb, page)
        ysbp, ysb = _sb_fused(qp, kt_bf, vt_bf, bias, q, k, v, ck_t, cv_t, page_table, bias_rows,
                              tq=_tile(s, SB_KEY_BLOCK), heads_per_step=min(nh, 4), t_new=t_new,
                              n_chunks=SAMPLE_CHUNKS, n_slots=SAMPLE_SLOTS)

        x3 = _mix_xattn(x1p, ycp.reshape(bp * s, dc), ysbp.reshape(bp * s, dsb), wout, wxq, wxo,
                        mk.reshape(bp, n_mem, -1), mv.reshape(bp, n_mem, -1), g, b,
                        alpha=alpha, tm=_tile(s, ROW_TILE), rows_per_mem=s)
        yp = ffn(x3, w2gu, w2dn, g[3:4], b[3:4], tm=_tile(bp * s, ROW_TILE))
        outs["kp"].append(kt.reshape(bp, nh, SB_HEAD_DIM, s).transpose(0, 3, 1, 2))
        outs["vp"].append(vt.reshape(bp, nh, SB_HEAD_DIM, s).transpose(0, 3, 1, 2))
        outs["cp"].append(cst)
        outs["mkp"].append(mk.reshape(bp, n_mem, -1, X_HEAD_DIM))
        outs["mvp"].append(mv.reshape(bp, n_mem, -1, X_HEAD_DIM))

        mem_per_step = 8

        def mem_rows(c):
            nxh = c.shape[2]
            c = c.reshape(db, n_mem, nxh, X_HEAD_DIM // 128, 128).transpose(0, 1, 3, 2, 4)
            return c.reshape(db, n_mem * nxh * (X_HEAD_DIM // 128), 128)

        x3 = _mix_xattn(x1, yc, ysb, wout, wxq, wxo, mem_rows(cache_mem_k[l]), mem_rows(cache_mem_v[l]),
                        g, b, alpha=alpha, tm=_tile(ms, mem_per_step * t_new), rows_per_mem=t_new)
        ys = ffn(x3, w2gu, w2dn, g[3:4], b[3:4], tm=_tile(ms, SAMPLE_ROW_TILE))
        outs["ks"].append(k.reshape(db, t_new, nh, SB_HEAD_DIM))
        outs["vs"].append(v.reshape(db, t_new, nh, SB_HEAD_DIM))
        outs["cs"].append(u.reshape(db, t_new, dc)[:, t_new - (CONV_WIDTH - 1):, :])

    st = lambda n: jnp.stack(outs[n])
    return (yp.reshape(bp, s, d), ys.reshape(db, t_new, d), st("kp"), st("vp"), st("cp"), st("mkp"),
            st("mvp"), st("ks"), st("vs"), st("cs"))
```
